```python
import math
import numpy as np
import jax, jax.numpy as jnp
from jax import lax

D_MODEL = 4096
BATCH = 4
SEQ = 2048
DEPTH = 2

GRID_W = 64
HEAD_DIM = 128
Q_BLOCK = 128
D_MIX = D_MODEL
GROUP_W = D_MIX // 4
NA_HEADS = GROUP_W // HEAD_DIM
NA_WIN_ROWS_MAX = 8
NA_WIN_COLS = 16
AX_HEADS = GROUP_W // HEAD_DIM
AX_KV_HEADS = AX_HEADS // 4
ROPE_DIM_PER_AXIS = HEAD_DIM // 2
ROPE_THETA = 10000.0
SW_HEADS = GROUP_W // HEAD_DIM
SW_KV_HEADS = SW_HEADS // 4
SW_WINDOW = 128
DF_V_DIM = 2 * HEAD_DIM
DF_HEADS = GROUP_W // DF_V_DIM
T5_BUCKETS = 32
T5_MAX_DIST = 128
T5_HEADS = SW_HEADS + DF_HEADS
D_FF = 4 * D_MODEL
EPS = 1e-6
IN_WIDTHS = (
    NA_HEADS * HEAD_DIM, NA_HEADS * HEAD_DIM, NA_HEADS * HEAD_DIM,
    AX_HEADS * HEAD_DIM, AX_KV_HEADS * HEAD_DIM, AX_KV_HEADS * HEAD_DIM,
    SW_HEADS * HEAD_DIM, SW_KV_HEADS * HEAD_DIM, SW_KV_HEADS * HEAD_DIM,
    2 * DF_HEADS * HEAD_DIM, 2 * DF_HEADS * HEAD_DIM, DF_HEADS * DF_V_DIM,
)
D_IN = sum(IN_WIDTHS)

kernel_name = "hybrid_parallel_head_group_encoder"

F32 = jnp.float32


def split_points():
    pts, acc = [], 0
    for w in IN_WIDTHS[:-1]:
        acc += w
        pts.append(acc)
    return pts


def rms_norm(x, g):
    xf = x.astype(F32)
    y = xf * lax.rsqrt(jnp.mean(xf * xf, axis=-1, keepdims=True) + EPS)
    return (y * g.astype(F32)).astype(x.dtype)


def t5_bucket(rel):
    nb = T5_BUCKETS // 2
    max_exact = nb // 2
    base = jnp.where(rel > 0, nb, 0)
    n = jnp.abs(rel)
    n_f = jnp.maximum(n, 1).astype(F32)
    large = max_exact + (jnp.log(n_f / max_exact) / math.log(T5_MAX_DIST / max_exact)
                         * (nb - max_exact)).astype(jnp.int32)
    large = jnp.minimum(large, nb - 1)
    return base + jnp.where(n < max_exact, n, large)


def neighbourhood_attention(q, k, v, rpb):
    B, S, H, Dh = q.shape
    rows = S // GRID_W
    kr = min(NA_WIN_ROWS_MAX, rows)
    kc = NA_WIN_COLS
    nk = kr * kc
    r = np.arange(rows)
    c = np.arange(GRID_W)
    rs = np.clip(r - kr // 2, 0, rows - kr)
    cs = np.clip(c - kc // 2, 0, GRID_W - kc)
    key_r = rs[:, None] + np.arange(kr)
    key_c = cs[:, None] + np.arange(kc)
    idx = (key_r[:, None, :, None] * GRID_W + key_c[None, :, None, :]).reshape(rows, GRID_W, nk)
    dr = np.broadcast_to((key_r - r[:, None])[:, None, :, None], (rows, GRID_W, kr, kc)) + NA_WIN_ROWS_MAX - 1
    dc = np.broadcast_to((key_c - c[:, None])[None, :, None, :], (rows, GRID_W, kr, kc)) + NA_WIN_COLS - 1
    bias = rpb[:, dr, dc].astype(F32).reshape(H, rows, GRID_W, nk).transpose(1, 0, 2, 3)
    q_rows = q.reshape(B, rows, GRID_W, H, Dh).transpose(1, 0, 3, 2, 4)
    scale = Dh ** -0.5

    def row_block(args):
        q_r, idx_r, bias_r = args
        k_g = k[:, idx_r]
        v_g = v[:, idx_r]
        s = jnp.einsum('bhwd,bwnhd->bhwn', q_r, k_g, preferred_element_type=F32) * scale + bias_r
        p = jax.nn.softmax(s, axis=-1)
        return jnp.einsum('bhwn,bwnhd->bwhd', p.astype(v.dtype), v_g)

    o = lax.map(row_block, (q_rows, jnp.asarray(idx, dtype=jnp.int32), bias))
    return o.transpose(1, 0, 2, 3, 4).reshape(B, S, H * Dh)


def axial_rope(S):
    pos = jnp.arange(S, dtype=jnp.int32)
    row = (pos // GRID_W).astype(F32)
    col = (pos % GRID_W).astype(F32)
    n_pairs = ROPE_DIM_PER_AXIS // 2
    inv = ROPE_THETA ** (-jnp.arange(n_pairs, dtype=F32) / n_pairs)
    ang = jnp.concatenate([row[:, None] * inv, col[:, None] * inv], axis=-1)
    return jnp.cos(ang), jnp.sin(ang)


def apply_rope(x, cos, sin):
    xf = x.astype(F32)
    x1, x2 = xf[..., 0::2], xf[..., 1::2]
    c, s = cos[None, :, None, :], sin[None, :, None, :]
    out = jnp.stack([x1 * c - x2 * s, x1 * s + x2 * c], axis=-1).reshape(x.shape)
    return out.astype(x.dtype)


def blocked_gqa(q, k, v):
    B, S, H, Dh = q.shape
    Hkv = k.shape[2]
    G = H // Hkv
    nb = S // Q_BLOCK
    scale = Dh ** -0.5
    qb = q.reshape(B, nb, Q_BLOCK, Hkv, G, Dh).transpose(1, 0, 2, 3, 4, 5)

    def step(qi):
        s = jnp.einsum('bqkgd,bskd->bkgqs', qi, k, preferred_element_type=F32) * scale
        p = jax.nn.softmax(s, axis=-1)
        return jnp.einsum('bkgqs,bskd->bqkgd', p.astype(v.dtype), v)

    o = lax.map(step, qb)
    return o.transpose(1, 0, 2, 3, 4, 5).reshape(B, S, H * Dh)


def sliding_window_gqa(q, k, v, sink, t5_tab):
    B, S, H, Dh = q.shape
    Hkv = k.shape[2]
    G = H // Hkv
    nb = S // Q_BLOCK
    n_side = SW_WINDOW // Q_BLOCK
    span = (2 * n_side + 1) * Q_BLOCK
    pad = n_side * Q_BLOCK
    scale = Dh ** -0.5

    def band(t):
        tp = jnp.pad(t, ((0, 0), (pad, pad), (0, 0), (0, 0)))
        tp = tp.reshape(B, nb + 2 * n_side, Q_BLOCK, Hkv, t.shape[-1])
        return jnp.concatenate([tp[:, j:j + nb] for j in range(2 * n_side + 1)], axis=2)

    kb, vb = band(k), band(v)
    qb = q.reshape(B, nb, Q_BLOCK, Hkv, G, Dh)
    j = jnp.arange(span, dtype=jnp.int32)
    t = jnp.arange(Q_BLOCK, dtype=jnp.int32)
    rel = j[None, :] - pad - t[:, None]
    kpos = jnp.arange(nb, dtype=jnp.int32)[:, None] * Q_BLOCK - pad + j[None, :]
    allowed = (jnp.abs(rel) <= SW_WINDOW)[None] & ((kpos >= 0) & (kpos < S))[:, None, :]
    bias = t5_tab[t5_bucket(rel)].astype(F32).transpose(2, 0, 1).reshape(Hkv, G, 1, Q_BLOCK, span)
    s = jnp.einsum('bnqkgd,bnjkd->bkgnqj', qb, kb, preferred_element_type=F32) * scale + bias
    s = jnp.where(allowed, s, -jnp.inf)
    sink_l = sink.astype(F32).reshape(Hkv, G, 1, 1, 1)
    m = jnp.maximum(jnp.max(s, axis=-1, keepdims=True), sink_l)
    e = jnp.exp(s - m)
    p = e / (jnp.sum(e, axis=-1, keepdims=True) + jnp.exp(sink_l - m))
    o = jnp.einsum('bkgnqj,bnjkd->bnqkgd', p.astype(v.dtype), vb)
    return o.reshape(B, S, H * Dh)


def differential_attention(q1, q2, k1, k2, v, lam, t5_tab):
    B, S, H, Dh = q1.shape
    nb = S // Q_BLOCK
    scale = Dh ** -0.5
    qb = jnp.stack([q1, q2], axis=0).reshape(2, B, nb, Q_BLOCK, H, Dh).transpose(2, 0, 1, 3, 4, 5)
    starts = jnp.arange(nb, dtype=jnp.int32) * Q_BLOCK
    kpos = jnp.arange(S, dtype=jnp.int32)

    def step(args):
        qi, q0 = args
        rel = kpos[None, :] - (q0 + jnp.arange(Q_BLOCK, dtype=jnp.int32))[:, None]
        bias = t5_tab[t5_bucket(rel)].astype(F32).transpose(2, 0, 1)
        s1 = jnp.einsum('bqhd,bshd->bhqs', qi[0], k1, preferred_element_type=F32) * scale + bias
        s2 = jnp.einsum('bqhd,bshd->bhqs', qi[1], k2, preferred_element_type=F32) * scale + bias
        p = jax.nn.softmax(s1, axis=-1) - lam * jax.nn.softmax(s2, axis=-1)
        return jnp.einsum('bhqs,bshe->bqhe', p.astype(v.dtype), v)

    o = lax.map(step, (qb, starts))
    return o.transpose(1, 0, 2, 3, 4).reshape(B, S, H, v.shape[-1])


def setup_inputs(seed: int = 0) -> dict:
    key = jax.random.key(seed)
    ks = jax.random.split(key, 18)

    def nrm(k, shape, scale):
        return jax.random.normal(k, shape, F32) * scale

    def gain(k, shape):
        return 1.0 + 0.05 * jax.random.normal(k, shape, F32)

    return {
        "x": nrm(ks[0], (BATCH, SEQ, D_MODEL), 1.0),
        "ln_attn_pre": gain(ks[1], (DEPTH, D_MODEL)),
        "ln_attn_post": gain(ks[2], (DEPTH, D_MODEL)),
        "ln_mlp_pre": gain(ks[3], (DEPTH, D_MODEL)),
        "ln_mlp_post": gain(ks[4], (DEPTH, D_MODEL)),
        "w_in": nrm(ks[5], (DEPTH, D_MODEL, D_IN), D_MODEL ** -0.5),
        "w_out": nrm(ks[6], (DEPTH, D_MIX, D_MODEL), D_MIX ** -0.5),
        "na_rpb": nrm(ks[7], (DEPTH, NA_HEADS, 2 * NA_WIN_ROWS_MAX - 1, 2 * NA_WIN_COLS - 1), 0.5),
        "ax_q_norm": gain(ks[8], (DEPTH, HEAD_DIM)),
        "ax_k_norm": gain(ks[9], (DEPTH, HEAD_DIM)),
        "sw_sink": nrm(ks[10], (DEPTH, SW_HEADS), 0.5),
        "df_lambda": nrm(ks[11], (DEPTH, 4, HEAD_DIM), 0.1),
        "df_subln": gain(ks[12], (DEPTH, DF_V_DIM)),
        "t5_table": nrm(ks[13], (T5_BUCKETS, T5_HEADS), 0.5),
        "w_mlp_in": nrm(ks[14], (DEPTH, D_MODEL, D_FF), D_MODEL ** -0.5),
        "w_mlp_out": nrm(ks[15], (DEPTH, D_FF, D_MODEL), D_FF ** -0.5),
    }


def reference(x, ln_attn_pre, ln_attn_post, ln_mlp_pre, ln_mlp_post, w_in, w_out, na_rpb,
              ax_q_norm, ax_k_norm, sw_sink, df_lambda, df_subln, t5_table, w_mlp_in, w_mlp_out):
    B, S, _ = x.shape
    cos, sin = axial_rope(S)
    pts = split_points()
    t5_sw = t5_table[:, :SW_HEADS]
    t5_df = t5_table[:, SW_HEADS:]
    for l in range(DEPTH):
        h = rms_norm(x, ln_attn_pre[l])
        proj = jnp.einsum('bsd,de->bse', h, w_in[l])
        aq, ak, av, bq, bk, bv, cq, ck, cv, dq, dk, dv = jnp.split(proj, pts, axis=-1)

        ya = neighbourhood_attention(aq.reshape(B, S, NA_HEADS, HEAD_DIM),
                                     ak.reshape(B, S, NA_HEADS, HEAD_DIM),
                                     av.reshape(B, S, NA_HEADS, HEAD_DIM), na_rpb[l])

        qb_ = apply_rope(rms_norm(bq.reshape(B, S, AX_HEADS, HEAD_DIM), ax_q_norm[l]), cos, sin)
        kb_ = apply_rope(rms_norm(bk.reshape(B, S, AX_KV_HEADS, HEAD_DIM), ax_k_norm[l]), cos, sin)
        yb = blocked_gqa(qb_, kb_, bv.reshape(B, S, AX_KV_HEADS, HEAD_DIM))

        yc = sliding_window_gqa(cq.reshape(B, S, SW_HEADS, HEAD_DIM),
                                ck.reshape(B, S, SW_KV_HEADS, HEAD_DIM),
                                cv.reshape(B, S, SW_KV_HEADS, HEAD_DIM), sw_sink[l], t5_sw)

        lambda_init = 0.8 - 0.6 * math.exp(-0.3 * l)
        lp = df_lambda[l].astype(F32)
        lam = jnp.exp(jnp.sum(lp[0] * lp[1])) - jnp.exp(jnp.sum(lp[2] * lp[3])) + lambda_init
        dq4 = dq.reshape(B, S, 2, DF_HEADS, HEAD_DIM)
        dk4 = dk.reshape(B, S, 2, DF_HEADS, HEAD_DIM)
        od = differential_attention(dq4[:, :, 0], dq4[:, :, 1], dk4[:, :, 0], dk4[:, :, 1],
                                    dv.reshape(B, S, DF_HEADS, DF_V_DIM), lam, t5_df)
        yd = (rms_norm(od, df_subln[l]) * (1.0 - lambda_init)).reshape(B, S, DF_HEADS * DF_V_DIM)

        mix = jnp.concatenate([ya, yb, yc, yd], axis=-1)
        x = x + rms_norm(jnp.einsum('bse,ed->bsd', mix, w_out[l]), ln_attn_post[l])

        h = rms_norm(x, ln_mlp_pre[l])
        u = jnp.square(jax.nn.relu(jnp.einsum('bsd,df->bsf', h, w_mlp_in[l])))
        x = x + rms_norm(jnp.einsum('bsf,fd->bsd', u, w_mlp_out[l]), ln_mlp_post[l])
    return x
```

```python
import functools
import math

import numpy as np
import jax
import jax.numpy as jnp
from jax import lax
from jax.experimental import pallas as pl
from jax.experimental.pallas import tpu as pltpu

F32 = jnp.float32
BF16 = jnp.bfloat16

D_MODEL = 4096
GRID_W = 64
HEAD_DIM = 128
Q_BLOCK = 128
GROUP_W = D_MODEL // 4
NA_HEADS = GROUP_W // HEAD_DIM
NA_WIN_ROWS = 8
NA_WIN_COLS = 16
AX_HEADS = GROUP_W // HEAD_DIM
AX_KV_HEADS = AX_HEADS // 4
ROPE_THETA = 10000.0
SW_HEADS = GROUP_W // HEAD_DIM
SW_KV_HEADS = SW_HEADS // 4
SW_WINDOW = 128
DF_V_DIM = 2 * HEAD_DIM
DF_HEADS = GROUP_W // DF_V_DIM
T5_BUCKETS = 32
T5_MAX_DIST = 128
T5_HEADS = SW_HEADS + DF_HEADS
D_FF = 4 * D_MODEL
EPS = 1e-6
SCALE = HEAD_DIM ** -0.5

OFF_AQ, OFF_AK, OFF_AV = 0, 1024, 2048
OFF_BQ, OFF_BK, OFF_BV = 3072, 4096, 4352
OFF_CQ, OFF_CK, OFF_CV = 4608, 5632, 5888
OFF_DQ, OFF_DK, OFF_DV = 6144, 7168, 8192
D_IN = 9216

NA_SPAN_BLOCKS = 5
NA_REL_BLOCKS = 9
VMEM_LIMIT = 56 * 1024 * 1024


def _cparams(sem):
    return pltpu.CompilerParams(dimension_semantics=sem, vmem_limit_bytes=VMEM_LIMIT)


def _dot(a, b):
    return jnp.dot(a, b, preferred_element_type=F32)


def _dot_nt(a, b):
    return lax.dot_general(a, b, (((1,), (1,)), ((), ())), preferred_element_type=F32)


def _rms(x, g):
    return x * lax.rsqrt(jnp.mean(x * x, axis=-1, keepdims=True) + EPS) * g


def _norm_cast_kernel(x_ref, g_ref, h_ref):
    h_ref[...] = _rms(x_ref[...], g_ref[...]).astype(h_ref.dtype)


def _norm_cast(x, g, rows=256):
    t, d = x.shape
    return pl.pallas_call(
        _norm_cast_kernel,
        grid=(t // rows,),
        in_specs=[pl.BlockSpec((rows, d), lambda i: (i, 0)),
                  pl.BlockSpec((1, d), lambda i: (0, 0))],
        out_specs=pl.BlockSpec((rows, d), lambda i: (i, 0)),
        out_shape=jax.ShapeDtypeStruct((t, d), BF16),
        compiler_params=_cparams(("parallel",)),
        name="norm_cast",
    )(x, g.reshape(1, d))


def _resid_norm_kernel(x_ref, y_ref, gp_ref, gn_ref, xo_ref, h_ref):
    xn = x_ref[...] + _rms(y_ref[...], gp_ref[...])
    xo_ref[...] = xn
    h_ref[...] = _rms(xn, gn_ref[...]).astype(h_ref.dtype)


def _resid_kernel(x_ref, y_ref, gp_ref, xo_ref):
    xo_ref[...] = x_ref[...] + _rms(y_ref[...], gp_ref[...])


def _resid_norm(x, y, g_post, g_next, rows=256):
    t, d = x.shape
    row_spec = pl.BlockSpec((rows, d), lambda i: (i, 0))
    g_spec = pl.BlockSpec((1, d), lambda i: (0, 0))
    if g_next is None:
        return pl.pallas_call(
            _resid_kernel, grid=(t // rows,),
            in_specs=[row_spec, row_spec, g_spec], out_specs=row_spec,
            out_shape=jax.ShapeDtypeStruct((t, d), F32),
            compiler_params=_cparams(("parallel",)), name="resid",
        )(x, y, g_post.reshape(1, d)), None
    return pl.pallas_call(
        _resid_norm_kernel, grid=(t // rows,),
        in_specs=[row_spec, row_spec, g_spec, g_spec], out_specs=[row_spec, row_spec],
        out_shape=[jax.ShapeDtypeStruct((t, d), F32), jax.ShapeDtypeStruct((t, d), BF16)],
        compiler_params=_cparams(("parallel",)), name="resid_norm",
    )(x, y, g_post.reshape(1, d), g_next.reshape(1, d))


def _relu2(r):
    r = jnp.maximum(r, 0.0)
    return r * r


def _mm_kernel(a_ref, w_ref, o_ref, *scratch, nk, act):
    if nk == 1:
        r = _dot(a_ref[...], w_ref[...])
        o_ref[...] = (act(r) if act else r).astype(o_ref.dtype)
        return
    acc_ref, = scratch
    k = pl.program_id(2)

    @pl.when(k == 0)
    def _():
        acc_ref[...] = _dot(a_ref[...], w_ref[...])

    @pl.when(k > 0)
    def _():
        acc_ref[...] += _dot(a_ref[...], w_ref[...])

    @pl.when(k == nk - 1)
    def _():
        r = acc_ref[...]
        o_ref[...] = (act(r) if act else r).astype(o_ref.dtype)


def _matmul(a, w, out_dtype, *, tm=1024, tn=1024, tk=None, act=None, name="matmul"):
    m, kdim = a.shape
    _, n = w.shape
    tk = kdim if tk is None else tk
    nk = kdim // tk
    scratch = [] if nk == 1 else [pltpu.VMEM((tm, tn), F32)]
    return pl.pallas_call(
        functools.partial(_mm_kernel, nk=nk, act=act),
        grid=(m // tm, n // tn, nk),
        in_specs=[pl.BlockSpec((tm, tk), lambda i, j, k: (i, k)),
                  pl.BlockSpec((tk, tn), lambda i, j, k: (k, j))],
        out_specs=pl.BlockSpec((tm, tn), lambda i, j, k: (i, j)),
        out_shape=jax.ShapeDtypeStruct((m, n), out_dtype),
        scratch_shapes=scratch,
        compiler_params=_cparams(("parallel", "parallel", "arbitrary")),
        name=name,
    )(a, w)


def _out_proj_kernel(ya_ref, yb_ref, yc_ref, yd_ref, w_ref, o_ref):
    acc = _dot(ya_ref[...], w_ref[0:GROUP_W, :])
    acc += _dot(yb_ref[...], w_ref[GROUP_W:2 * GROUP_W, :])
    acc += _dot(yc_ref[...], w_ref[2 * GROUP_W:3 * GROUP_W, :])
    acc += _dot(yd_ref[...], w_ref[3 * GROUP_W:4 * GROUP_W, :])
    o_ref[...] = acc


def _out_proj(ya, yb, yc, yd, w, tm=1024, tn=1024):
    m = ya.shape[0]
    kdim, n = w.shape
    a_spec = pl.BlockSpec((tm, GROUP_W), lambda i, j: (i, 0))
    return pl.pallas_call(
        _out_proj_kernel,
        grid=(m // tm, n // tn),
        in_specs=[a_spec, a_spec, a_spec, a_spec,
                  pl.BlockSpec((kdim, tn), lambda i, j: (0, j))],
        out_specs=pl.BlockSpec((tm, tn), lambda i, j: (i, j)),
        out_shape=jax.ShapeDtypeStruct((m, n), F32),
        compiler_params=_cparams(("parallel", "parallel")),
        name="out_proj",
    )(ya, yb, yc, yd, w)


def _t5_bucket(rel):
    nb = T5_BUCKETS // 2
    max_exact = nb // 2
    base = jnp.where(rel > 0, nb, 0)
    n = jnp.abs(rel)
    n_f = jnp.maximum(n, 1).astype(F32)
    large = max_exact + (jnp.log(n_f / max_exact) / math.log(T5_MAX_DIST / max_exact)
                         * (nb - max_exact)).astype(jnp.int32)
    large = jnp.minimum(large, nb - 1)
    return base + jnp.where(n < max_exact, n, large)


def _t5_band_kernel(idx_ref, tab_ref, o_ref):
    h = pl.program_id(0)
    idx = idx_ref[...]
    acc = jnp.zeros(idx.shape, F32)
    for b in range(T5_BUCKETS):
        acc = jnp.where(idx == b, tab_ref[b, h], acc)
    o_ref[0] = acc


def _t5_band(t5_table):
    span = 3 * Q_BLOCK
    j = jnp.arange(span, dtype=jnp.int32)
    t = jnp.arange(Q_BLOCK, dtype=jnp.int32)
    idx = _t5_bucket(j[None, :] - Q_BLOCK - t[:, None]).astype(jnp.int32)
    return pl.pallas_call(
        _t5_band_kernel,
        grid=(T5_HEADS,),
        in_specs=[pl.BlockSpec((Q_BLOCK, span), lambda h: (0, 0)),
                  pl.BlockSpec(memory_space=pltpu.SMEM)],
        out_specs=pl.BlockSpec((1, Q_BLOCK, span), lambda h: (h, 0, 0)),
        out_shape=jax.ShapeDtypeStruct((T5_HEADS, Q_BLOCK, span), F32),
        compiler_params=_cparams(("arbitrary",)),
        name="t5_band",
    )(idx, t5_table)


NA_RPB_ROWS = 2 * NA_WIN_ROWS - 1
NA_RPB_COLS = 2 * NA_WIN_COLS - 1


def _na_table_kernel(rpb_ref, o_ref):
    h = pl.program_id(0)
    dblk = pl.program_id(1) - (NA_REL_BLOCKS // 2)
    t = lax.broadcasted_iota(jnp.int32, (Q_BLOCK, Q_BLOCK), 0)
    u = lax.broadcasted_iota(jnp.int32, (Q_BLOCK, Q_BLOCK), 1)
    q_hi = t >= GRID_W
    k_hi = u >= GRID_W
    c = jnp.bitwise_and(t, GRID_W - 1)
    kc = jnp.bitwise_and(u, GRID_W - 1)
    cs = jnp.clip(c - NA_WIN_COLS // 2, 0, GRID_W - NA_WIN_COLS)
    col_ok = (kc >= cs) & (kc < cs + NA_WIN_COLS)
    dc = kc - c + (NA_WIN_COLS - 1)
    neg = jnp.float32(-jnp.inf)

    def entry(dr, d):
        ok = (dr >= -(NA_WIN_ROWS - 1)) & (dr <= NA_WIN_ROWS - 1)
        row = jnp.clip(dr + NA_WIN_ROWS - 1, 0, NA_RPB_ROWS - 1)
        v = rpb_ref[h * (NA_RPB_ROWS * NA_RPB_COLS) + row * NA_RPB_COLS + d]
        return jnp.where(ok, v, neg)

    val = jnp.full((Q_BLOCK, Q_BLOCK), neg, F32)
    for d in range(NA_RPB_COLS):
        v00 = entry(2 * dblk, d)
        v01 = entry(2 * dblk + 1, d)
        v10 = entry(2 * dblk - 1, d)
        v11 = entry(2 * dblk, d)
        vm = jnp.where(q_hi, jnp.where(k_hi, v11, v10), jnp.where(k_hi, v01, v00))
        val = jnp.where((dc == d) & col_ok, vm, val)
    o_ref[0, 0] = val


def _na_table(rpb):
    return pl.pallas_call(
        _na_table_kernel,
        grid=(NA_HEADS, NA_REL_BLOCKS),
        in_specs=[pl.BlockSpec(memory_space=pltpu.SMEM)],
        out_specs=pl.BlockSpec((1, 1, Q_BLOCK, Q_BLOCK), lambda h, d: (h, d, 0, 0)),
        out_shape=jax.ShapeDtypeStruct((NA_HEADS, NA_REL_BLOCKS, Q_BLOCK, Q_BLOCK), F32),
        compiler_params=_cparams(("arbitrary", "arbitrary")),
        name="na_table",
    )(rpb.reshape(-1))


def _attn_a_kernel(q_ref, k_ref, v_ref, tb_ref, o_ref, *, n_blocks):
    i = pl.program_id(2)
    sb = jnp.clip(i - 2, 0, n_blocks - NA_SPAN_BLOCKS)
    start = pl.multiple_of(sb * Q_BLOCK, Q_BLOCK)
    span = NA_SPAN_BLOCKS * Q_BLOCK
    q = q_ref[0]
    kspan = k_ref[0, pl.ds(start, span), :]
    vspan = v_ref[0, pl.ds(start, span), :]
    s = _dot_nt(q, kspan) * SCALE
    bias = jnp.concatenate(
        [tb_ref[0, sb + j - i + NA_REL_BLOCKS // 2] for j in range(NA_SPAN_BLOCKS)], axis=1)
    t = lax.broadcasted_iota(jnp.int32, (Q_BLOCK, span), 0)
    u = lax.broadcasted_iota(jnp.int32, (Q_BLOCK, span), 1)
    rows_total = 2 * n_blocks
    r = 2 * i + jnp.where(t >= GRID_W, 1, 0)
    kr = 2 * sb + lax.shift_right_logical(u, 6)
    rs = jnp.clip(r - NA_WIN_ROWS // 2, 0, rows_total - NA_WIN_ROWS)
    valid = (kr >= rs) & (kr < rs + NA_WIN_ROWS)
    s = jnp.where(valid, s + bias, -jnp.inf)
    m = jnp.max(s, axis=-1, keepdims=True)
    e = jnp.exp(s - m)
    l = jnp.sum(e, axis=-1, keepdims=True)
    o = _dot(e.astype(BF16), vspan) / l
    o_ref[0] = o.astype(o_ref.dtype)


def _attn_a(proj, table):
    b, s, _ = proj.shape
    nb = s // Q_BLOCK
    hb = HEAD_DIM
    return pl.pallas_call(
        functools.partial(_attn_a_kernel, n_blocks=nb),
        grid=(NA_HEADS, b, nb),
        in_specs=[
            pl.BlockSpec((1, Q_BLOCK, hb), lambda h, bb, i: (bb, i, OFF_AQ // hb + h)),
            pl.BlockSpec((1, s, hb), lambda h, bb, i: (bb, 0, OFF_AK // hb + h)),
            pl.BlockSpec((1, s, hb), lambda h, bb, i: (bb, 0, OFF_AV // hb + h)),
            pl.BlockSpec((1, NA_REL_BLOCKS, Q_BLOCK, Q_BLOCK), lambda h, bb, i: (h, 0, 0, 0)),
        ],
        out_specs=pl.BlockSpec((1, Q_BLOCK, hb), lambda h, bb, i: (bb, i, h)),
        out_shape=jax.ShapeDtypeStruct((b, s, GROUP_W), BF16),
        compiler_params=_cparams(("parallel", "parallel", "arbitrary")),
        name="attn_a",
    )(proj, proj, proj, table)


def _rope(x, cos, sin_signed):
    lane = lax.broadcasted_iota(jnp.int32, x.shape, 1)
    even = jnp.bitwise_and(lane, 1) == 0
    swapped = jnp.where(even, pltpu.roll(x, HEAD_DIM - 1, 1), pltpu.roll(x, 1, 1))
    return x * cos + swapped * sin_signed


def _attn_b_kernel(q_ref, k_ref, v_ref, cos_ref, sin_ref, qg_ref, kg_ref, o_ref, kt_ref):
    i = pl.program_id(2)
    groups = AX_HEADS // AX_KV_HEADS

    @pl.when(i == 0)
    def _():
        k = _rope(_rms(k_ref[0].astype(F32), kg_ref[...]), cos_ref[...], sin_ref[...])
        kt_ref[...] = k.T.astype(BF16)

    row0 = pl.multiple_of(i * Q_BLOCK, Q_BLOCK)
    cos = cos_ref[pl.ds(row0, Q_BLOCK), :]
    sin = sin_ref[pl.ds(row0, Q_BLOCK), :]
    qs = []
    for g in range(groups):
        qh = q_ref[0, :, g * HEAD_DIM:(g + 1) * HEAD_DIM].astype(F32)
        qs.append(_rope(_rms(qh, qg_ref[...]), cos, sin).astype(BF16))
    q = jnp.concatenate(qs, axis=0)
    s = _dot(q, kt_ref[...]) * SCALE
    m = jnp.max(s, axis=-1, keepdims=True)
    e = jnp.exp(s - m)
    l = jnp.sum(e, axis=-1, keepdims=True)
    o = _dot(e.astype(BF16), v_ref[0]) / l
    o_ref[0] = jnp.concatenate(
        [o[g * Q_BLOCK:(g + 1) * Q_BLOCK] for g in range(groups)], axis=1).astype(o_ref.dtype)


def _attn_b(proj, cos, sin_signed, q_gain, k_gain):
    b, s, _ = proj.shape
    nb = s // Q_BLOCK
    groups = AX_HEADS // AX_KV_HEADS
    qw = groups * HEAD_DIM
    return pl.pallas_call(
        _attn_b_kernel,
        grid=(b, AX_KV_HEADS, nb),
        in_specs=[
            pl.BlockSpec((1, Q_BLOCK, qw), lambda bb, kv, i: (bb, i, OFF_BQ // qw + kv)),
            pl.BlockSpec((1, s, HEAD_DIM), lambda bb, kv, i: (bb, 0, OFF_BK // HEAD_DIM + kv)),
            pl.BlockSpec((1, s, HEAD_DIM), lambda bb, kv, i: (bb, 0, OFF_BV // HEAD_DIM + kv)),
            pl.BlockSpec((s, HEAD_DIM), lambda bb, kv, i: (0, 0)),
            pl.BlockSpec((s, HEAD_DIM), lambda bb, kv, i: (0, 0)),
            pl.BlockSpec((1, HEAD_DIM), lambda bb, kv, i: (0, 0)),
            pl.BlockSpec((1, HEAD_DIM), lambda bb, kv, i: (0, 0)),
        ],
        out_specs=pl.BlockSpec((1, Q_BLOCK, qw), lambda bb, kv, i: (bb, i, kv)),
        out_shape=jax.ShapeDtypeStruct((b, s, GROUP_W), BF16),
        scratch_shapes=[pltpu.VMEM((HEAD_DIM, s), BF16)],
        compiler_params=_cparams(("parallel", "parallel", "arbitrary")),
        name="attn_b",
    )(proj, proj, proj, cos, sin_signed, q_gain.reshape(1, -1), k_gain.reshape(1, -1))


def _attn_c_kernel(q_ref, kp_ref, kc_ref, kn_ref, vp_ref, vc_ref, vn_ref, band_ref, sink_ref,
                   o_ref, *, n_blocks):
    kv = pl.program_id(1)
    i = pl.program_id(2)
    groups = SW_HEADS // SW_KV_HEADS
    span = 3 * Q_BLOCK
    q = jnp.concatenate(
        [q_ref[0, :, g * HEAD_DIM:(g + 1) * HEAD_DIM] for g in range(groups)], axis=0)
    s = jnp.concatenate(
        [_dot_nt(q, kp_ref[0]), _dot_nt(q, kc_ref[0]), _dot_nt(q, kn_ref[0])], axis=1) * SCALE
    bias = jnp.concatenate([band_ref[g] for g in range(groups)], axis=0)
    rows = groups * Q_BLOCK
    t = jnp.bitwise_and(lax.broadcasted_iota(jnp.int32, (rows, span), 0), Q_BLOCK - 1)
    u = lax.broadcasted_iota(jnp.int32, (rows, span), 1)
    rel = u - Q_BLOCK - t
    kpos = (i - 1) * Q_BLOCK + u
    allowed = (jnp.abs(rel) <= SW_WINDOW) & (kpos >= 0) & (kpos < n_blocks * Q_BLOCK)
    s = jnp.where(allowed, s + bias, -jnp.inf)
    sink = jnp.concatenate(
        [jnp.full((Q_BLOCK, 1), sink_ref[kv * groups + g], F32) for g in range(groups)], axis=0)
    m = jnp.maximum(jnp.max(s, axis=-1, keepdims=True), sink)
    e = jnp.exp(s - m)
    l = jnp.sum(e, axis=-1, keepdims=True) + jnp.exp(sink - m)
    v = jnp.concatenate([vp_ref[0], vc_ref[0], vn_ref[0]], axis=0)
    o = _dot(e.astype(BF16), v) / l
    o_ref[0] = jnp.concatenate(
        [o[g * Q_BLOCK:(g + 1) * Q_BLOCK] for g in range(groups)], axis=1).astype(o_ref.dtype)


def _attn_c(proj, band, sink):
    b, s, _ = proj.shape
    nb = s // Q_BLOCK
    groups = SW_HEADS // SW_KV_HEADS
    qw = groups * HEAD_DIM
    hd = HEAD_DIM

    def kv_specs(off):
        col = off // hd
        return [
            pl.BlockSpec((1, Q_BLOCK, hd), lambda bb, kv, i: (bb, jnp.maximum(i - 1, 0), col + kv)),
            pl.BlockSpec((1, Q_BLOCK, hd), lambda bb, kv, i: (bb, i, col + kv)),
            pl.BlockSpec((1, Q_BLOCK, hd), lambda bb, kv, i: (bb, jnp.minimum(i + 1, nb - 1), col + kv)),
        ]

    return pl.pallas_call(
        functools.partial(_attn_c_kernel, n_blocks=nb),
        grid=(b, SW_KV_HEADS, nb),
        in_specs=[pl.BlockSpec((1, Q_BLOCK, qw), lambda bb, kv, i: (bb, i, OFF_CQ // qw + kv))]
        + kv_specs(OFF_CK) + kv_specs(OFF_CV)
        + [pl.BlockSpec((groups, Q_BLOCK, 3 * Q_BLOCK), lambda bb, kv, i: (kv, 0, 0)),
           pl.BlockSpec(memory_space=pltpu.SMEM)],
        out_specs=pl.BlockSpec((1, Q_BLOCK, qw), lambda bb, kv, i: (bb, i, kv)),
        out_shape=jax.ShapeDtypeStruct((b, s, GROUP_W), BF16),
        compiler_params=_cparams(("parallel", "parallel", "arbitrary")),
        name="attn_c",
    )(proj, proj, proj, proj, proj, proj, proj, band, sink)


def _attn_d_kernel(q1_ref, q2_ref, k1_ref, k2_ref, v_ref, band_ref, tab_ref, lam_ref, g_ref,
                   o_ref, k1t_ref, k2t_ref, *, n_blocks, lambda_init):
    h = pl.program_id(1)
    i = pl.program_id(2)

    @pl.when(i == 0)
    def _():
        k1t_ref[...] = k1_ref[0].astype(F32).T.astype(BF16)
        k2t_ref[...] = k2_ref[0].astype(F32).T.astype(BF16)

    lp = lam_ref[...]
    lam = (jnp.exp(jnp.sum(lp[0:1] * lp[1:2], axis=-1, keepdims=True))
           - jnp.exp(jnp.sum(lp[2:3] * lp[3:4], axis=-1, keepdims=True)) + lambda_init)

    far_lo = tab_ref[T5_BUCKETS // 2 - 1, SW_HEADS + h]
    far_hi = tab_ref[T5_BUCKETS - 1, SW_HEADS + h]
    pieces = []
    for kj in range(n_blocks):
        far = jnp.full((Q_BLOCK, Q_BLOCK), jnp.where(kj < i, far_lo, far_hi), F32)
        piece = jnp.where(kj == i - 1, band_ref[0, :, 0:Q_BLOCK], far)
        piece = jnp.where(kj == i, band_ref[0, :, Q_BLOCK:2 * Q_BLOCK], piece)
        piece = jnp.where(kj == i + 1, band_ref[0, :, 2 * Q_BLOCK:3 * Q_BLOCK], piece)
        pieces.append(piece)
    bias = jnp.concatenate(pieces, axis=1)

    def softmax_parts(q_ref, kt_ref):
        s = _dot(q_ref[0], kt_ref[...]) * SCALE + bias
        m = jnp.max(s, axis=-1, keepdims=True)
        e = jnp.exp(s - m)
        return e, jnp.sum(e, axis=-1, keepdims=True)

    e1, l1 = softmax_parts(q1_ref, k1t_ref)
    e2, l2 = softmax_parts(q2_ref, k2t_ref)
    p = e1 * (1.0 / l1) - e2 * (lam / l2)
    od = _dot(p.astype(BF16), v_ref[0])
    o_ref[0] = (_rms(od, g_ref[...]) * (1.0 - lambda_init)).astype(o_ref.dtype)


def _attn_d(proj, band, t5_table, lam_params, subln, lambda_init):
    b, s, _ = proj.shape
    nb = s // Q_BLOCK
    hd = HEAD_DIM
    return pl.pallas_call(
        functools.partial(_attn_d_kernel, n_blocks=nb, lambda_init=lambda_init),
        grid=(b, DF_HEADS, nb),
        in_specs=[
            pl.BlockSpec((1, Q_BLOCK, hd), lambda bb, h, i: (bb, i, OFF_DQ // hd + h)),
            pl.BlockSpec((1, Q_BLOCK, hd), lambda bb, h, i: (bb, i, OFF_DQ // hd + DF_HEADS + h)),
            pl.BlockSpec((1, s, hd), lambda bb, h, i: (bb, 0, OFF_DK // hd + h)),
            pl.BlockSpec((1, s, hd), lambda bb, h, i: (bb, 0, OFF_DK // hd + DF_HEADS + h)),
            pl.BlockSpec((1, s, DF_V_DIM), lambda bb, h, i: (bb, 0, OFF_DV // DF_V_DIM + h)),
            pl.BlockSpec((1, Q_BLOCK, 3 * Q_BLOCK), lambda bb, h, i: (SW_HEADS + h, 0, 0)),
            pl.BlockSpec(memory_space=pltpu.SMEM),
            pl.BlockSpec((4, hd), lambda bb, h, i: (0, 0)),
            pl.BlockSpec((1, DF_V_DIM), lambda bb, h, i: (0, 0)),
        ],
        out_specs=pl.BlockSpec((1, Q_BLOCK, DF_V_DIM), lambda bb, h, i: (bb, i, h)),
        out_shape=jax.ShapeDtypeStruct((b, s, GROUP_W), BF16),
        scratch_shapes=[pltpu.VMEM((hd, s), BF16), pltpu.VMEM((hd, s), BF16)],
        compiler_params=_cparams(("parallel", "parallel", "arbitrary")),
        name="attn_d",
    )(proj, proj, proj, proj, proj, band, t5_table, lam_params, subln.reshape(1, -1))


def _rope_tables(s):
    pos = jnp.arange(s, dtype=jnp.int32)
    row = (pos // GRID_W).astype(F32)
    col = (pos % GRID_W).astype(F32)
    n_pairs = HEAD_DIM // 4
    inv = ROPE_THETA ** (-jnp.arange(n_pairs, dtype=F32) / n_pairs)
    ang = jnp.concatenate([row[:, None] * inv, col[:, None] * inv], axis=-1)
    cos = jnp.repeat(jnp.cos(ang), 2, axis=-1)
    sin = jnp.repeat(jnp.sin(ang), 2, axis=-1)
    sign = jnp.tile(jnp.asarray([-1.0, 1.0], F32), HEAD_DIM // 2)
    return cos, sin * sign


def kernel(x, ln_attn_pre, ln_attn_post, ln_mlp_pre, ln_mlp_post, w_in, w_out, na_rpb,
           ax_q_norm, ax_k_norm, sw_sink, df_lambda, df_subln, t5_table, w_mlp_in, w_mlp_out):
    b, s, d = x.shape
    depth = w_in.shape[0]
    t = b * s
    cos, sin_signed = _rope_tables(s)
    band = _t5_band(t5_table)
    xf = x.reshape(t, d)
    h = _norm_cast(xf, ln_attn_pre[0])
    for l in range(depth):
        proj = _matmul(h, w_in[l].astype(BF16), BF16, name="in_proj").reshape(b, s, D_IN)
        ya = _attn_a(proj, _na_table(na_rpb[l]))
        yb = _attn_b(proj, cos, sin_signed, ax_q_norm[l], ax_k_norm[l])
        yc = _attn_c(proj, band, sw_sink[l])
        lambda_init = 0.8 - 0.6 * math.exp(-0.3 * l)
        yd = _attn_d(proj, band, t5_table, df_lambda[l], df_subln[l], lambda_init)
        y = _out_proj(ya.reshape(t, -1), yb.reshape(t, -1), yc.reshape(t, -1), yd.reshape(t, -1),
                      w_out[l].astype(BF16))
        xf, h = _resid_norm(xf, y, ln_attn_post[l], ln_mlp_pre[l])
        u = _matmul(h, w_mlp_in[l].astype(BF16), BF16, act=_relu2, name="mlp_in")
        y = _matmul(u, w_mlp_out[l].astype(BF16), F32, tk=2048, name="mlp_out")
        g_next = ln_attn_pre[l + 1] if l + 1 < depth else None
        xf, h = _resid_norm(xf, y, ln_mlp_post[l], g_next)
    return xf.reshape(b, s, d)
```

```python
import functools
import math

import numpy as np
import jax
import jax.numpy as jnp
from jax import lax
from jax.experimental import pallas as pl
from jax.experimental.pallas import tpu as pltpu

F32 = jnp.float32
BF16 = jnp.bfloat16

D_MODEL = 4096
GRID_W = 64
HEAD_DIM = 128
Q_BLOCK = 128
GROUP_W = D_MODEL // 4
NA_HEADS = GROUP_W // HEAD_DIM
NA_WIN_ROWS = 8
NA_WIN_COLS = 16
AX_HEADS = GROUP_W // HEAD_DIM
AX_KV_HEADS = AX_HEADS // 4
ROPE_THETA = 10000.0
SW_HEADS = GROUP_W // HEAD_DIM
SW_KV_HEADS = SW_HEADS // 4
SW_WINDOW = 128
DF_V_DIM = 2 * HEAD_DIM
DF_HEADS = GROUP_W // DF_V_DIM
T5_BUCKETS = 32
T5_MAX_DIST = 128
T5_HEADS = SW_HEADS + DF_HEADS
D_FF = 4 * D_MODEL
EPS = 1e-6
SCALE = HEAD_DIM ** -0.5

OFF_AQ, OFF_AK, OFF_AV = 0, 1024, 2048
OFF_BQ, OFF_BK, OFF_BV = 3072, 4096, 4352
OFF_CQ, OFF_CK, OFF_CV = 4608, 5632, 5888
OFF_DQ, OFF_DK, OFF_DV = 6144, 7168, 8192
D_IN = 9216

NA_SPAN_BLOCKS = 5
NA_REL_BLOCKS = 9
VMEM_LIMIT = 56 * 1024 * 1024


def _cparams(sem):
    return pltpu.CompilerParams(dimension_semantics=sem, vmem_limit_bytes=VMEM_LIMIT)


def _dot(a, b):
    return jnp.dot(a, b, preferred_element_type=F32)


def _dot_nt(a, b):
    return lax.dot_general(a, b, (((1,), (1,)), ((), ())), preferred_element_type=F32)


def _rms(x, g):
    return x * lax.rsqrt(jnp.mean(x * x, axis=-1, keepdims=True) + EPS) * g


def _norm_cast_kernel(x_ref, g_ref, h_ref):
    h_ref[...] = _rms(x_ref[...], g_ref[...]).astype(h_ref.dtype)


def _norm_cast(x, g, rows=256):
    t, d = x.shape
    return pl.pallas_call(
        _norm_cast_kernel,
        grid=(t // rows,),
        in_specs=[pl.BlockSpec((rows, d), lambda i: (i, 0)),
                  pl.BlockSpec((1, d), lambda i: (0, 0))],
        out_specs=pl.BlockSpec((rows, d), lambda i: (i, 0)),
        out_shape=jax.ShapeDtypeStruct((t, d), BF16),
        compiler_params=_cparams(("parallel",)),
        name="norm_cast",
    )(x, g.reshape(1, d))


def _resid_norm_kernel(x_ref, y_ref, gp_ref, gn_ref, xo_ref, h_ref):
    xn = x_ref[...] + _rms(y_ref[...], gp_ref[...])
    xo_ref[...] = xn
    h_ref[...] = _rms(xn, gn_ref[...]).astype(h_ref.dtype)


def _resid_kernel(x_ref, y_ref, gp_ref, xo_ref):
    xo_ref[...] = x_ref[...] + _rms(y_ref[...], gp_ref[...])


def _resid_norm(x, y, g_post, g_next, rows=256):
    t, d = x.shape
    row_spec = pl.BlockSpec((rows, d), lambda i: (i, 0))
    g_spec = pl.BlockSpec((1, d), lambda i: (0, 0))
    if g_next is None:
        return pl.pallas_call(
            _resid_kernel, grid=(t // rows,),
            in_specs=[row_spec, row_spec, g_spec], out_specs=row_spec,
            out_shape=jax.ShapeDtypeStruct((t, d), F32),
            compiler_params=_cparams(("parallel",)), name="resid",
        )(x, y, g_post.reshape(1, d)), None
    return pl.pallas_call(
        _resid_norm_kernel, grid=(t // rows,),
        in_specs=[row_spec, row_spec, g_spec, g_spec], out_specs=[row_spec, row_spec],
        out_shape=[jax.ShapeDtypeStruct((t, d), F32), jax.ShapeDtypeStruct((t, d), BF16)],
        compiler_params=_cparams(("parallel",)), name="resid_norm",
    )(x, y, g_post.reshape(1, d), g_next.reshape(1, d))


def _relu2(r):
    r = jnp.maximum(r, 0.0)
    return r * r


def _mm_kernel(a_ref, w_ref, o_ref, *scratch, nk, act):
    w = w_ref[...].astype(BF16)
    if nk == 1:
        r = _dot(a_ref[...], w)
        o_ref[...] = (act(r) if act else r).astype(o_ref.dtype)
        return
    acc_ref, = scratch
    k = pl.program_id(2)

    @pl.when(k == 0)
    def _():
        acc_ref[...] = _dot(a_ref[...], w)

    @pl.when(k > 0)
    def _():
        acc_ref[...] += _dot(a_ref[...], w)

    @pl.when(k == nk - 1)
    def _():
        r = acc_ref[...]
        o_ref[...] = (act(r) if act else r).astype(o_ref.dtype)


def _matmul(a, w, layer, out_dtype, *, tm=1024, tn=512, tk=None, act=None, name="matmul"):
    m, kdim = a.shape
    _, _, n = w.shape
    tk = kdim if tk is None else tk
    nk = kdim // tk
    scratch = [] if nk == 1 else [pltpu.VMEM((tm, tn), F32)]
    return pl.pallas_call(
        functools.partial(_mm_kernel, nk=nk, act=act),
        grid=(m // tm, n // tn, nk),
        in_specs=[pl.BlockSpec((tm, tk), lambda i, j, k: (i, k)),
                  pl.BlockSpec((None, tk, tn), lambda i, j, k: (layer, k, j))],
        out_specs=pl.BlockSpec((tm, tn), lambda i, j, k: (i, j)),
        out_shape=jax.ShapeDtypeStruct((m, n), out_dtype),
        scratch_shapes=scratch,
        compiler_params=_cparams(("parallel", "parallel", "arbitrary")),
        name=name,
    )(a, w)


def _out_proj_kernel(ya_ref, yb_ref, yc_ref, yd_ref, w_ref, o_ref):
    acc = _dot(ya_ref[...], w_ref[0:GROUP_W, :].astype(BF16))
    acc += _dot(yb_ref[...], w_ref[GROUP_W:2 * GROUP_W, :].astype(BF16))
    acc += _dot(yc_ref[...], w_ref[2 * GROUP_W:3 * GROUP_W, :].astype(BF16))
    acc += _dot(yd_ref[...], w_ref[3 * GROUP_W:4 * GROUP_W, :].astype(BF16))
    o_ref[...] = acc


def _out_proj(ya, yb, yc, yd, w, layer, tm=1024, tn=512):
    m = ya.shape[0]
    _, kdim, n = w.shape
    a_spec = pl.BlockSpec((tm, GROUP_W), lambda i, j: (i, 0))
    return pl.pallas_call(
        _out_proj_kernel,
        grid=(m // tm, n // tn),
        in_specs=[a_spec, a_spec, a_spec, a_spec,
                  pl.BlockSpec((None, kdim, tn), lambda i, j: (layer, 0, j))],
        out_specs=pl.BlockSpec((tm, tn), lambda i, j: (i, j)),
        out_shape=jax.ShapeDtypeStruct((m, n), F32),
        compiler_params=_cparams(("parallel", "parallel")),
        name="out_proj",
    )(ya, yb, yc, yd, w)


def _t5_bucket(rel):
    nb = T5_BUCKETS // 2
    max_exact = nb // 2
    base = jnp.where(rel > 0, nb, 0)
    n = jnp.abs(rel)
    n_f = jnp.maximum(n, 1).astype(F32)
    large = max_exact + (jnp.log(n_f / max_exact) / math.log(T5_MAX_DIST / max_exact)
                         * (nb - max_exact)).astype(jnp.int32)
    large = jnp.minimum(large, nb - 1)
    return base + jnp.where(n < max_exact, n, large)


def _t5_band_kernel(idx_ref, tab_ref, o_ref):
    h = pl.program_id(0)
    idx = idx_ref[...]
    acc = jnp.zeros(idx.shape, F32)
    for b in range(T5_BUCKETS):
        acc = jnp.where(idx == b, tab_ref[b, h], acc)
    o_ref[0] = acc


def _t5_band(t5_table):
    span = 3 * Q_BLOCK
    j = jnp.arange(span, dtype=jnp.int32)
    t = jnp.arange(Q_BLOCK, dtype=jnp.int32)
    idx = _t5_bucket(j[None, :] - Q_BLOCK - t[:, None]).astype(jnp.int32)
    return pl.pallas_call(
        _t5_band_kernel,
        grid=(T5_HEADS,),
        in_specs=[pl.BlockSpec((Q_BLOCK, span), lambda h: (0, 0)),
                  pl.BlockSpec(memory_space=pltpu.SMEM)],
        out_specs=pl.BlockSpec((1, Q_BLOCK, span), lambda h: (h, 0, 0)),
        out_shape=jax.ShapeDtypeStruct((T5_HEADS, Q_BLOCK, span), F32),
        compiler_params=_cparams(("arbitrary",)),
        name="t5_band",
    )(idx, t5_table)


NA_RPB_ROWS = 2 * NA_WIN_ROWS - 1
NA_RPB_COLS = 2 * NA_WIN_COLS - 1


def _na_table_kernel(rpb_ref, o_ref):
    h = pl.program_id(0)
    dblk = pl.program_id(1) - (NA_REL_BLOCKS // 2)
    t = lax.broadcasted_iota(jnp.int32, (Q_BLOCK, Q_BLOCK), 0)
    u = lax.broadcasted_iota(jnp.int32, (Q_BLOCK, Q_BLOCK), 1)
    q_hi = t >= GRID_W
    k_hi = u >= GRID_W
    c = jnp.bitwise_and(t, GRID_W - 1)
    kc = jnp.bitwise_and(u, GRID_W - 1)
    cs = jnp.clip(c - NA_WIN_COLS // 2, 0, GRID_W - NA_WIN_COLS)
    col_ok = (kc >= cs) & (kc < cs + NA_WIN_COLS)
    dc = kc - c + (NA_WIN_COLS - 1)
    neg = jnp.float32(-jnp.inf)

    def entry(dr, d):
        ok = (dr >= -(NA_WIN_ROWS - 1)) & (dr <= NA_WIN_ROWS - 1)
        row = jnp.clip(dr + NA_WIN_ROWS - 1, 0, NA_RPB_ROWS - 1)
        v = rpb_ref[h * (NA_RPB_ROWS * NA_RPB_COLS) + row * NA_RPB_COLS + d]
        return jnp.where(ok, v, neg)

    val = jnp.full((Q_BLOCK, Q_BLOCK), neg, F32)
    for d in range(NA_RPB_COLS):
        v00 = entry(2 * dblk, d)
        v01 = entry(2 * dblk + 1, d)
        v10 = entry(2 * dblk - 1, d)
        v11 = entry(2 * dblk, d)
        vm = jnp.where(q_hi, jnp.where(k_hi, v11, v10), jnp.where(k_hi, v01, v00))
        val = jnp.where((dc == d) & col_ok, vm, val)
    o_ref[0, 0] = val


def _na_table(rpb):
    return pl.pallas_call(
        _na_table_kernel,
        grid=(NA_HEADS, NA_REL_BLOCKS),
        in_specs=[pl.BlockSpec(memory_space=pltpu.SMEM)],
        out_specs=pl.BlockSpec((1, 1, Q_BLOCK, Q_BLOCK), lambda h, d: (h, d, 0, 0)),
        out_shape=jax.ShapeDtypeStruct((NA_HEADS, NA_REL_BLOCKS, Q_BLOCK, Q_BLOCK), F32),
        compiler_params=_cparams(("arbitrary", "arbitrary")),
        name="na_table",
    )(rpb.reshape(-1))


def _attn_a_kernel(q_ref, k_ref, v_ref, tb_ref, o_ref, *, n_blocks):
    i = pl.program_id(2)
    sb = jnp.clip(i - 2, 0, n_blocks - NA_SPAN_BLOCKS)
    start = pl.multiple_of(sb * Q_BLOCK, Q_BLOCK)
    span = NA_SPAN_BLOCKS * Q_BLOCK
    q = q_ref[0]
    kspan = k_ref[0, pl.ds(start, span), :]
    vspan = v_ref[0, pl.ds(start, span), :]
    s = _dot_nt(q, kspan) * SCALE
    bias = jnp.concatenate(
        [tb_ref[0, sb + j - i + NA_REL_BLOCKS // 2] for j in range(NA_SPAN_BLOCKS)], axis=1)
    t = lax.broadcasted_iota(jnp.int32, (Q_BLOCK, span), 0)
    u = lax.broadcasted_iota(jnp.int32, (Q_BLOCK, span), 1)
    rows_total = 2 * n_blocks
    r = 2 * i + jnp.where(t >= GRID_W, 1, 0)
    kr = 2 * sb + lax.shift_right_logical(u, 6)
    rs = jnp.clip(r - NA_WIN_ROWS // 2, 0, rows_total - NA_WIN_ROWS)
    valid = (kr >= rs) & (kr < rs + NA_WIN_ROWS)
    s = jnp.where(valid, s + bias, -jnp.inf)
    m = jnp.max(s, axis=-1, keepdims=True)
    e = jnp.exp(s - m)
    l = jnp.sum(e, axis=-1, keepdims=True)
    o = _dot(e.astype(BF16), vspan) / l
    o_ref[0] = o.astype(o_ref.dtype)


def _attn_a(proj, table):
    b, s, _ = proj.shape
    nb = s // Q_BLOCK
    hb = HEAD_DIM
    return pl.pallas_call(
        functools.partial(_attn_a_kernel, n_blocks=nb),
        grid=(NA_HEADS, b, nb),
        in_specs=[
            pl.BlockSpec((1, Q_BLOCK, hb), lambda h, bb, i: (bb, i, OFF_AQ // hb + h)),
            pl.BlockSpec((1, s, hb), lambda h, bb, i: (bb, 0, OFF_AK // hb + h)),
            pl.BlockSpec((1, s, hb), lambda h, bb, i: (bb, 0, OFF_AV // hb + h)),
            pl.BlockSpec((1, NA_REL_BLOCKS, Q_BLOCK, Q_BLOCK), lambda h, bb, i: (h, 0, 0, 0)),
        ],
        out_specs=pl.BlockSpec((1, Q_BLOCK, hb), lambda h, bb, i: (bb, i, h)),
        out_shape=jax.ShapeDtypeStruct((b, s, GROUP_W), BF16),
        compiler_params=_cparams(("parallel", "parallel", "arbitrary")),
        name="attn_a",
    )(proj, proj, proj, table)


def _rope(x, cos, sin_signed):
    lane = lax.broadcasted_iota(jnp.int32, x.shape, 1)
    even = jnp.bitwise_and(lane, 1) == 0
    swapped = jnp.where(even, pltpu.roll(x, HEAD_DIM - 1, 1), pltpu.roll(x, 1, 1))
    return x * cos + swapped * sin_signed


def _attn_b_kernel(q_ref, k_ref, v_ref, cos_ref, sin_ref, qg_ref, kg_ref, o_ref, kt_ref):
    i = pl.program_id(2)
    groups = AX_HEADS // AX_KV_HEADS

    @pl.when(i == 0)
    def _():
        k = _rope(_rms(k_ref[0].astype(F32), kg_ref[...]), cos_ref[...], sin_ref[...])
        kt_ref[...] = k.T.astype(BF16)

    row0 = pl.multiple_of(i * Q_BLOCK, Q_BLOCK)
    cos = cos_ref[pl.ds(row0, Q_BLOCK), :]
    sin = sin_ref[pl.ds(row0, Q_BLOCK), :]
    qs = []
    for g in range(groups):
        qh = q_ref[0, :, g * HEAD_DIM:(g + 1) * HEAD_DIM].astype(F32)
        qs.append(_rope(_rms(qh, qg_ref[...]), cos, sin).astype(BF16))
    q = jnp.concatenate(qs, axis=0)
    s = _dot(q, kt_ref[...]) * SCALE
    m = jnp.max(s, axis=-1, keepdims=True)
    e = jnp.exp(s - m)
    l = jnp.sum(e, axis=-1, keepdims=True)
    o = _dot(e.astype(BF16), v_ref[0]) / l
    o_ref[0] = jnp.concatenate(
        [o[g * Q_BLOCK:(g + 1) * Q_BLOCK] for g in range(groups)], axis=1).astype(o_ref.dtype)


def _attn_b(proj, cos, sin_signed, q_gain, k_gain):
    b, s, _ = proj.shape
    nb = s // Q_BLOCK
    groups = AX_HEADS // AX_KV_HEADS
    qw = groups * HEAD_DIM
    return pl.pallas_call(
        _attn_b_kernel,
        grid=(b, AX_KV_HEADS, nb),
        in_specs=[
            pl.BlockSpec((1, Q_BLOCK, qw), lambda bb, kv, i: (bb, i, OFF_BQ // qw + kv)),
            pl.BlockSpec((1, s, HEAD_DIM), lambda bb, kv, i: (bb, 0, OFF_BK // HEAD_DIM + kv)),
            pl.BlockSpec((1, s, HEAD_DIM), lambda bb, kv, i: (bb, 0, OFF_BV // HEAD_DIM + kv)),
            pl.BlockSpec((s, HEAD_DIM), lambda bb, kv, i: (0, 0)),
            pl.BlockSpec((s, HEAD_DIM), lambda bb, kv, i: (0, 0)),
            pl.BlockSpec((1, HEAD_DIM), lambda bb, kv, i: (0, 0)),
            pl.BlockSpec((1, HEAD_DIM), lambda bb, kv, i: (0, 0)),
        ],
        out_specs=pl.BlockSpec((1, Q_BLOCK, qw), lambda bb, kv, i: (bb, i, kv)),
        out_shape=jax.ShapeDtypeStruct((b, s, GROUP_W), BF16),
        scratch_shapes=[pltpu.VMEM((HEAD_DIM, s), BF16)],
        compiler_params=_cparams(("parallel", "parallel", "arbitrary")),
        name="attn_b",
    )(proj, proj, proj, cos, sin_signed, q_gain.reshape(1, -1), k_gain.reshape(1, -1))


def _attn_c_kernel(q_ref, kp_ref, kc_ref, kn_ref, vp_ref, vc_ref, vn_ref, band_ref, sink_ref,
                   o_ref, *, n_blocks):
    kv = pl.program_id(1)
    i = pl.program_id(2)
    groups = SW_HEADS // SW_KV_HEADS
    span = 3 * Q_BLOCK
    q = jnp.concatenate(
        [q_ref[0, :, g * HEAD_DIM:(g + 1) * HEAD_DIM] for g in range(groups)], axis=0)
    s = jnp.concatenate(
        [_dot_nt(q, kp_ref[0]), _dot_nt(q, kc_ref[0]), _dot_nt(q, kn_ref[0])], axis=1) * SCALE
    bias = jnp.concatenate([band_ref[g] for g in range(groups)], axis=0)
    rows = groups * Q_BLOCK
    t = jnp.bitwise_and(lax.broadcasted_iota(jnp.int32, (rows, span), 0), Q_BLOCK - 1)
    u = lax.broadcasted_iota(jnp.int32, (rows, span), 1)
    rel = u - Q_BLOCK - t
    kpos = (i - 1) * Q_BLOCK + u
    allowed = (jnp.abs(rel) <= SW_WINDOW) & (kpos >= 0) & (kpos < n_blocks * Q_BLOCK)
    s = jnp.where(allowed, s + bias, -jnp.inf)
    sink = jnp.concatenate(
        [jnp.full((Q_BLOCK, 1), sink_ref[kv * groups + g], F32) for g in range(groups)], axis=0)
    m = jnp.maximum(jnp.max(s, axis=-1, keepdims=True), sink)
    e = jnp.exp(s - m)
    l = jnp.sum(e, axis=-1, keepdims=True) + jnp.exp(sink - m)
    v = jnp.concatenate([vp_ref[0], vc_ref[0], vn_ref[0]], axis=0)
    o = _dot(e.astype(BF16), v) / l
    o_ref[0] = jnp.concatenate(
        [o[g * Q_BLOCK:(g + 1) * Q_BLOCK] for g in range(groups)], axis=1).astype(o_ref.dtype)


def _attn_c(proj, band, sink):
    b, s, _ = proj.shape
    nb = s // Q_BLOCK
    groups = SW_HEADS // SW_KV_HEADS
    qw = groups * HEAD_DIM
    hd = HEAD_DIM

    def kv_specs(off):
        col = off // hd
        return [
            pl.BlockSpec((1, Q_BLOCK, hd), lambda bb, kv, i: (bb, jnp.maximum(i - 1, 0), col + kv)),
            pl.BlockSpec((1, Q_BLOCK, hd), lambda bb, kv, i: (bb, i, col + kv)),
            pl.BlockSpec((1, Q_BLOCK, hd), lambda bb, kv, i: (bb, jnp.minimum(i + 1, nb - 1), col + kv)),
        ]

    return pl.pallas_call(
        functools.partial(_attn_c_kernel, n_blocks=nb),
        grid=(b, SW_KV_HEADS, nb),
        in_specs=[pl.BlockSpec((1, Q_BLOCK, qw), lambda bb, kv, i: (bb, i, OFF_CQ // qw + kv))]
        + kv_specs(OFF_CK) + kv_specs(OFF_CV)
        + [pl.BlockSpec((groups, Q_BLOCK, 3 * Q_BLOCK), lambda bb, kv, i: (kv, 0, 0)),
           pl.BlockSpec(memory_space=pltpu.SMEM)],
        out_specs=pl.BlockSpec((1, Q_BLOCK, qw), lambda bb, kv, i: (bb, i, kv)),
        out_shape=jax.ShapeDtypeStruct((b, s, GROUP_W), BF16),
        compiler_params=_cparams(("parallel", "parallel", "arbitrary")),
        name="attn_c",
    )(proj, proj, proj, proj, proj, proj, proj, band, sink)


def _attn_d_kernel(q1_ref, q2_ref, k1_ref, k2_ref, v_ref, band_ref, tab_ref, lam_ref, g_ref,
                   o_ref, k1t_ref, k2t_ref, *, n_blocks, lambda_init):
    h = pl.program_id(1)
    i = pl.program_id(2)

    @pl.when(i == 0)
    def _():
        k1t_ref[...] = k1_ref[0].astype(F32).T.astype(BF16)
        k2t_ref[...] = k2_ref[0].astype(F32).T.astype(BF16)

    lp = lam_ref[...]
    lam = (jnp.exp(jnp.sum(lp[0:1] * lp[1:2], axis=-1, keepdims=True))
           - jnp.exp(jnp.sum(lp[2:3] * lp[3:4], axis=-1, keepdims=True)) + lambda_init)

    far_lo = tab_ref[T5_BUCKETS // 2 - 1, SW_HEADS + h]
    far_hi = tab_ref[T5_BUCKETS - 1, SW_HEADS + h]
    pieces = []
    for kj in range(n_blocks):
        far = jnp.full((Q_BLOCK, Q_BLOCK), jnp.where(kj < i, far_lo, far_hi), F32)
        piece = jnp.where(kj == i - 1, band_ref[0, :, 0:Q_BLOCK], far)
        piece = jnp.where(kj == i, band_ref[0, :, Q_BLOCK:2 * Q_BLOCK], piece)
        piece = jnp.where(kj == i + 1, band_ref[0, :, 2 * Q_BLOCK:3 * Q_BLOCK], piece)
        pieces.append(piece)
    bias = jnp.concatenate(pieces, axis=1)

    def softmax_parts(q_ref, kt_ref):
        s = _dot(q_ref[0], kt_ref[...]) * SCALE + bias
        m = jnp.max(s, axis=-1, keepdims=True)
        e = jnp.exp(s - m)
        return e, jnp.sum(e, axis=-1, keepdims=True)

    e1, l1 = softmax_parts(q1_ref, k1t_ref)
    e2, l2 = softmax_parts(q2_ref, k2t_ref)
    p = e1 * (1.0 / l1) - e2 * (lam / l2)
    od = _dot(p.astype(BF16), v_ref[0])
    o_ref[0] = (_rms(od, g_ref[...]) * (1.0 - lambda_init)).astype(o_ref.dtype)


def _attn_d(proj, band, t5_table, lam_params, subln, lambda_init):
    b, s, _ = proj.shape
    nb = s // Q_BLOCK
    hd = HEAD_DIM
    return pl.pallas_call(
        functools.partial(_attn_d_kernel, n_blocks=nb, lambda_init=lambda_init),
        grid=(b, DF_HEADS, nb),
        in_specs=[
            pl.BlockSpec((1, Q_BLOCK, hd), lambda bb, h, i: (bb, i, OFF_DQ // hd + h)),
            pl.BlockSpec((1, Q_BLOCK, hd), lambda bb, h, i: (bb, i, OFF_DQ // hd + DF_HEADS + h)),
            pl.BlockSpec((1, s, hd), lambda bb, h, i: (bb, 0, OFF_DK // hd + h)),
            pl.BlockSpec((1, s, hd), lambda bb, h, i: (bb, 0, OFF_DK // hd + DF_HEADS + h)),
            pl.BlockSpec((1, s, DF_V_DIM), lambda bb, h, i: (bb, 0, OFF_DV // DF_V_DIM + h)),
            pl.BlockSpec((1, Q_BLOCK, 3 * Q_BLOCK), lambda bb, h, i: (SW_HEADS + h, 0, 0)),
            pl.BlockSpec(memory_space=pltpu.SMEM),
            pl.BlockSpec((4, hd), lambda bb, h, i: (0, 0)),
            pl.BlockSpec((1, DF_V_DIM), lambda bb, h, i: (0, 0)),
        ],
        out_specs=pl.BlockSpec((1, Q_BLOCK, DF_V_DIM), lambda bb, h, i: (bb, i, h)),
        out_shape=jax.ShapeDtypeStruct((b, s, GROUP_W), BF16),
        scratch_shapes=[pltpu.VMEM((hd, s), BF16), pltpu.VMEM((hd, s), BF16)],
        compiler_params=_cparams(("parallel", "parallel", "arbitrary")),
        name="attn_d",
    )(proj, proj, proj, proj, proj, band, t5_table, lam_params, subln.reshape(1, -1))


def _rope_tables(s):
    pos = jnp.arange(s, dtype=jnp.int32)
    row = (pos // GRID_W).astype(F32)
    col = (pos % GRID_W).astype(F32)
    n_pairs = HEAD_DIM // 4
    inv = ROPE_THETA ** (-jnp.arange(n_pairs, dtype=F32) / n_pairs)
    ang = jnp.concatenate([row[:, None] * inv, col[:, None] * inv], axis=-1)
    cos = jnp.repeat(jnp.cos(ang), 2, axis=-1)
    sin = jnp.repeat(jnp.sin(ang), 2, axis=-1)
    sign = jnp.tile(jnp.asarray([-1.0, 1.0], F32), HEAD_DIM // 2)
    return cos, sin * sign


def kernel(x, ln_attn_pre, ln_attn_post, ln_mlp_pre, ln_mlp_post, w_in, w_out, na_rpb,
           ax_q_norm, ax_k_norm, sw_sink, df_lambda, df_subln, t5_table, w_mlp_in, w_mlp_out):
    b, s, d = x.shape
    depth = w_in.shape[0]
    t = b * s
    cos, sin_signed = _rope_tables(s)
    band = _t5_band(t5_table)
    xf = x.reshape(t, d)
    h = _norm_cast(xf, ln_attn_pre[0])
    for l in range(depth):
        proj = _matmul(h, w_in, l, BF16, name="in_proj").reshape(b, s, D_IN)
        ya = _attn_a(proj, _na_table(na_rpb[l]))
        yb = _attn_b(proj, cos, sin_signed, ax_q_norm[l], ax_k_norm[l])
        yc = _attn_c(proj, band, sw_sink[l])
        lambda_init = 0.8 - 0.6 * math.exp(-0.3 * l)
        yd = _attn_d(proj, band, t5_table, df_lambda[l], df_subln[l], lambda_init)
        y = _out_proj(ya.reshape(t, -1), yb.reshape(t, -1), yc.reshape(t, -1), yd.reshape(t, -1),
                      w_out, l)
        xf, h = _resid_norm(xf, y, ln_attn_post[l], ln_mlp_pre[l])
        u = _matmul(h, w_mlp_in, l, BF16, act=_relu2, name="mlp_in")
        y = _matmul(u, w_mlp_out, l, F32, tn=1024, tk=2048, name="mlp_out")
        g_next = ln_attn_pre[l + 1] if l + 1 < depth else None
        xf, h = _resid_norm(xf, y, ln_mlp_post[l], g_next)
    return xf.reshape(b, s, d)
```

```python
import functools
import math

import jax
import jax.numpy as jnp
from jax import lax
from jax.experimental import pallas as pl
from jax.experimental.pallas import tpu as pltpu

F32 = jnp.float32
BF16 = jnp.bfloat16

D_MODEL = 4096
GRID_W = 64
HEAD_DIM = 128
Q_BLOCK = 128
GROUP_W = D_MODEL // 4
NA_HEADS = GROUP_W // HEAD_DIM
NA_WIN_ROWS = 8
NA_WIN_COLS = 16
AX_HEADS = GROUP_W // HEAD_DIM
AX_KV_HEADS = AX_HEADS // 4
ROPE_THETA = 10000.0
SW_HEADS = GROUP_W // HEAD_DIM
SW_KV_HEADS = SW_HEADS // 4
SW_WINDOW = 128
DF_V_DIM = 2 * HEAD_DIM
DF_HEADS = GROUP_W // DF_V_DIM
T5_BUCKETS = 32
T5_MAX_DIST = 128
T5_HEADS = SW_HEADS + DF_HEADS
EPS = 1e-6
SCALE = HEAD_DIM ** -0.5
LOG2E = math.log2(math.e)
LOGIT_SCALE = SCALE * LOG2E

OFF_AQ, OFF_AK, OFF_AV = 0, 1024, 2048
OFF_BQ, OFF_BK, OFF_BV = 3072, 4096, 4352
OFF_CQ, OFF_CK, OFF_CV = 4608, 5632, 5888
OFF_DQ, OFF_DK, OFF_DV = 6144, 7168, 8192
D_IN = 9216

KT_A, KT_B, KT_C, KT_D = 0, NA_HEADS, NA_HEADS + AX_KV_HEADS, NA_HEADS + AX_KV_HEADS + SW_KV_HEADS
KT_HEADS = KT_D + 2 * DF_HEADS

NA_SPAN_BLOCKS = 5
NA_REL_BLOCKS = 9
NA_RPB_ROWS = 2 * NA_WIN_ROWS - 1
NA_RPB_COLS = 2 * NA_WIN_COLS - 1

T5_FAR_LO, T5_BAND0, T5_FAR_HI, T5_MASKED, T5_KINDS = 0, 1, 4, 5, 6

VMEM_LIMIT = 56 * 1024 * 1024
NEG_INF = float("-inf")


def _cparams(sem):
    return pltpu.CompilerParams(dimension_semantics=sem, vmem_limit_bytes=VMEM_LIMIT)


def _dot(a, b):
    return jnp.dot(a, b, preferred_element_type=F32)


def _rms(x, g):
    return x * lax.rsqrt(jnp.mean(x * x, axis=-1, keepdims=True) + EPS) * g


def _softmax_parts(s, extra=None):
    m = jnp.max(s, axis=-1, keepdims=True)
    if extra is not None:
        m = jnp.maximum(m, extra)
    e = jnp.exp2(s - m)
    l = jnp.sum(e, axis=-1, keepdims=True)
    if extra is not None:
        l = l + jnp.exp2(extra - m)
    return e, l


def _norm_cast_kernel(x_ref, g_ref, h_ref):
    h_ref[...] = _rms(x_ref[...], g_ref[...]).astype(h_ref.dtype)


def _norm_cast(x, g, rows=256):
    t, d = x.shape
    return pl.pallas_call(
        _norm_cast_kernel,
        grid=(t // rows,),
        in_specs=[pl.BlockSpec((rows, d), lambda i: (i, 0)),
                  pl.BlockSpec((1, d), lambda i: (0, 0))],
        out_specs=pl.BlockSpec((rows, d), lambda i: (i, 0)),
        out_shape=jax.ShapeDtypeStruct((t, d), BF16),
        compiler_params=_cparams(("parallel",)),
        name="norm_cast",
    )(x, g.reshape(1, d))


def _resid_norm_kernel(x_ref, y_ref, gp_ref, gn_ref, xo_ref, h_ref):
    xn = x_ref[...] + _rms(y_ref[...], gp_ref[...])
    xo_ref[...] = xn
    h_ref[...] = _rms(xn, gn_ref[...]).astype(h_ref.dtype)


def _resid_kernel(x_ref, y_ref, gp_ref, xo_ref):
    xo_ref[...] = x_ref[...] + _rms(y_ref[...], gp_ref[...])


def _resid_norm(x, y, g_post, g_next, rows=256):
    t, d = x.shape
    row_spec = pl.BlockSpec((rows, d), lambda i: (i, 0))
    g_spec = pl.BlockSpec((1, d), lambda i: (0, 0))
    if g_next is None:
        return pl.pallas_call(
            _resid_kernel, grid=(t // rows,),
            in_specs=[row_spec, row_spec, g_spec], out_specs=row_spec,
            out_shape=jax.ShapeDtypeStruct((t, d), F32),
            compiler_params=_cparams(("parallel",)), name="resid",
        )(x, y, g_post.reshape(1, d)), None
    return pl.pallas_call(
        _resid_norm_kernel, grid=(t // rows,),
        in_specs=[row_spec, row_spec, g_spec, g_spec], out_specs=[row_spec, row_spec],
        out_shape=[jax.ShapeDtypeStruct((t, d), F32), jax.ShapeDtypeStruct((t, d), BF16)],
        compiler_params=_cparams(("parallel",)), name="resid_norm",
    )(x, y, g_post.reshape(1, d), g_next.reshape(1, d))


def _relu2(r):
    r = jnp.maximum(r, 0.0)
    return r * r


def _mm_kernel(a_ref, w_ref, *rest, nk, act, col_scaled):
    cs_ref = rest[0] if col_scaled else None
    o_ref = rest[1] if col_scaled else rest[0]
    w = w_ref[...].astype(BF16)

    def finish(r):
        if act is not None:
            r = act(r)
        if col_scaled:
            r = r * cs_ref[...]
        o_ref[...] = r.astype(o_ref.dtype)

    if nk == 1:
        finish(_dot(a_ref[...], w))
        return
    acc_ref = rest[-1]
    k = pl.program_id(2)

    @pl.when(k == 0)
    def _():
        acc_ref[...] = _dot(a_ref[...], w)

    @pl.when(k > 0)
    def _():
        acc_ref[...] += _dot(a_ref[...], w)

    @pl.when(k == nk - 1)
    def _():
        finish(acc_ref[...])


def _matmul(a, w, layer, out_dtype, *, tm=1024, tn=512, tk=None, act=None, col_scale=None,
            name="matmul"):
    m, kdim = a.shape
    _, _, n = w.shape
    tk = kdim if tk is None else tk
    nk = kdim // tk
    scratch = [] if nk == 1 else [pltpu.VMEM((tm, tn), F32)]
    in_specs = [pl.BlockSpec((tm, tk), lambda i, j, k: (i, k)),
                pl.BlockSpec((None, tk, tn), lambda i, j, k: (layer, k, j))]
    args = [a, w]
    if col_scale is not None:
        in_specs.append(pl.BlockSpec((1, tn), lambda i, j, k: (0, j)))
        args.append(col_scale)
    return pl.pallas_call(
        functools.partial(_mm_kernel, nk=nk, act=act, col_scaled=col_scale is not None),
        grid=(m // tm, n // tn, nk),
        in_specs=in_specs,
        out_specs=pl.BlockSpec((tm, tn), lambda i, j, k: (i, j)),
        out_shape=jax.ShapeDtypeStruct((m, n), out_dtype),
        scratch_shapes=scratch,
        compiler_params=_cparams(("parallel", "parallel", "arbitrary")),
        name=name,
    )(*args)


def _out_proj_kernel(ya_ref, yb_ref, yc_ref, yd_ref, w_ref, o_ref):
    acc = _dot(ya_ref[...], w_ref[0:GROUP_W, :].astype(BF16))
    acc += _dot(yb_ref[...], w_ref[GROUP_W:2 * GROUP_W, :].astype(BF16))
    acc += _dot(yc_ref[...], w_ref[2 * GROUP_W:3 * GROUP_W, :].astype(BF16))
    acc += _dot(yd_ref[...], w_ref[3 * GROUP_W:4 * GROUP_W, :].astype(BF16))
    o_ref[...] = acc


def _out_proj(ya, yb, yc, yd, w, layer, tm=1024, tn=512):
    m = ya.shape[0]
    _, kdim, n = w.shape
    a_spec = pl.BlockSpec((tm, GROUP_W), lambda i, j: (i, 0))
    return pl.pallas_call(
        _out_proj_kernel,
        grid=(m // tm, n // tn),
        in_specs=[a_spec, a_spec, a_spec, a_spec,
                  pl.BlockSpec((None, kdim, tn), lambda i, j: (layer, 0, j))],
        out_specs=pl.BlockSpec((tm, tn), lambda i, j: (i, j)),
        out_shape=jax.ShapeDtypeStruct((m, n), F32),
        compiler_params=_cparams(("parallel", "parallel")),
        name="out_proj",
    )(ya, yb, yc, yd, w)


def _t5_bucket(rel):
    nb = T5_BUCKETS // 2
    max_exact = nb // 2
    base = jnp.where(rel > 0, nb, 0)
    n = jnp.abs(rel)
    n_f = jnp.maximum(n, 1).astype(F32)
    large = max_exact + (jnp.log(n_f / max_exact) / math.log(T5_MAX_DIST / max_exact)
                         * (nb - max_exact)).astype(jnp.int32)
    large = jnp.minimum(large, nb - 1)
    return base + jnp.where(n < max_exact, n, large)


def _t5_blocks_kernel(idx_ref, tab_ref, o_ref):
    h = pl.program_id(0)
    kind = pl.program_id(1)
    idx = idx_ref[0]
    band = jnp.zeros(idx.shape, F32)
    for b in range(T5_BUCKETS):
        band = jnp.where(idx == b, tab_ref[b, h], band)
    t = lax.broadcasted_iota(jnp.int32, idx.shape, 0)
    u = lax.broadcasted_iota(jnp.int32, idx.shape, 1)
    rel = u + (kind - (T5_BAND0 + 1)) * Q_BLOCK - t
    allowed = (jnp.abs(rel) <= SW_WINDOW) | (h >= SW_HEADS)
    val = jnp.where(allowed, band, NEG_INF)
    val = jnp.where(kind == T5_FAR_LO, tab_ref[T5_BUCKETS // 2 - 1, h], val)
    val = jnp.where(kind == T5_FAR_HI, tab_ref[T5_BUCKETS - 1, h], val)
    val = jnp.where(kind == T5_MASKED, NEG_INF, val)
    o_ref[0, 0] = val * LOG2E


def _t5_blocks(t5_table):
    t = jnp.arange(Q_BLOCK, dtype=jnp.int32)
    rel = (jnp.arange(3 * Q_BLOCK, dtype=jnp.int32)[None, :] - Q_BLOCK - t[:, None])
    idx = _t5_bucket(rel).astype(jnp.int32).reshape(Q_BLOCK, 3, Q_BLOCK).transpose(1, 0, 2)
    return pl.pallas_call(
        _t5_blocks_kernel,
        grid=(T5_HEADS, T5_KINDS),
        in_specs=[pl.BlockSpec((1, Q_BLOCK, Q_BLOCK),
                               lambda h, kind: (jnp.clip(kind - T5_BAND0, 0, 2), 0, 0)),
                  pl.BlockSpec(memory_space=pltpu.SMEM)],
        out_specs=pl.BlockSpec((1, 1, Q_BLOCK, Q_BLOCK), lambda h, kind: (h, kind, 0, 0)),
        out_shape=jax.ShapeDtypeStruct((T5_HEADS, T5_KINDS, Q_BLOCK, Q_BLOCK), F32),
        compiler_params=_cparams(("arbitrary", "arbitrary")),
        name="t5_blocks",
    )(idx, t5_table)


def _na_table_kernel(rpb_ref, o_ref):
    h = pl.program_id(0)
    dblk = pl.program_id(1) - (NA_REL_BLOCKS // 2)
    t = lax.broadcasted_iota(jnp.int32, (Q_BLOCK, Q_BLOCK), 0)
    u = lax.broadcasted_iota(jnp.int32, (Q_BLOCK, Q_BLOCK), 1)
    q_hi = t >= GRID_W
    k_hi = u >= GRID_W
    c = jnp.bitwise_and(t, GRID_W - 1)
    kc = jnp.bitwise_and(u, GRID_W - 1)
    cs = jnp.clip(c - NA_WIN_COLS // 2, 0, GRID_W - NA_WIN_COLS)
    col_ok = (kc >= cs) & (kc < cs + NA_WIN_COLS)
    dc = kc - c + (NA_WIN_COLS - 1)
    neg = jnp.float32(NEG_INF)

    def entry(dr, d):
        ok = (dr >= -(NA_WIN_ROWS - 1)) & (dr <= NA_WIN_ROWS - 1)
        row = jnp.clip(dr + NA_WIN_ROWS - 1, 0, NA_RPB_ROWS - 1)
        v = rpb_ref[h * (NA_RPB_ROWS * NA_RPB_COLS) + row * NA_RPB_COLS + d]
        return jnp.where(ok, v, neg)

    val = jnp.full((Q_BLOCK, Q_BLOCK), neg, F32)
    for d in range(NA_RPB_COLS):
        v00 = entry(2 * dblk, d)
        v01 = entry(2 * dblk + 1, d)
        v10 = entry(2 * dblk - 1, d)
        v11 = entry(2 * dblk, d)
        vm = jnp.where(q_hi, jnp.where(k_hi, v11, v10), jnp.where(k_hi, v01, v00))
        val = jnp.where((dc == d) & col_ok, vm, val)
    o_ref[0, 0] = val * LOG2E


def _na_table(rpb):
    return pl.pallas_call(
        _na_table_kernel,
        grid=(NA_HEADS, NA_REL_BLOCKS),
        in_specs=[pl.BlockSpec(memory_space=pltpu.SMEM)],
        out_specs=pl.BlockSpec((1, 1, Q_BLOCK, Q_BLOCK), lambda h, d: (h, d, 0, 0)),
        out_shape=jax.ShapeDtypeStruct((NA_HEADS, NA_REL_BLOCKS, Q_BLOCK, Q_BLOCK), F32),
        compiler_params=_cparams(("arbitrary", "arbitrary")),
        name="na_table",
    )(rpb.reshape(-1))


def _rope(x, cos, sin_signed):
    lane = lax.broadcasted_iota(jnp.int32, x.shape, 1)
    even = jnp.bitwise_and(lane, 1) == 0
    swapped = jnp.where(even, pltpu.roll(x, HEAD_DIM - 1, 1), pltpu.roll(x, 1, 1))
    return x * cos + swapped * sin_signed


def _prep_kernel(ak_ref, bq_ref, bk_ref, ck_ref, dk_ref, cos_ref, sin_ref, qg_ref, kg_ref,
                 qo_ref, kt_ref):
    cos = cos_ref[...]
    sin = sin_ref[...]

    def head(ref, h):
        return ref[0, :, h * HEAD_DIM:(h + 1) * HEAD_DIM].astype(F32)

    for h in range(AX_HEADS):
        qh = _rope(_rms(head(bq_ref, h), qg_ref[...]), cos, sin)
        qo_ref[0, :, h * HEAD_DIM:(h + 1) * HEAD_DIM] = (qh * LOGIT_SCALE).astype(qo_ref.dtype)
    for h in range(NA_HEADS):
        kt_ref[0, KT_A + h] = head(ak_ref, h).T.astype(kt_ref.dtype)
    for h in range(AX_KV_HEADS):
        kh = _rope(_rms(head(bk_ref, h), kg_ref[...]), cos, sin)
        kt_ref[0, KT_B + h] = kh.T.astype(kt_ref.dtype)
    for h in range(SW_KV_HEADS):
        kt_ref[0, KT_C + h] = head(ck_ref, h).T.astype(kt_ref.dtype)
    for h in range(2 * DF_HEADS):
        kt_ref[0, KT_D + h] = head(dk_ref, h).T.astype(kt_ref.dtype)


def _prep(proj, cos, sin_signed, q_gain, k_gain, rows=512):
    b, s, _ = proj.shape
    kwb = AX_KV_HEADS * HEAD_DIM
    kwc = SW_KV_HEADS * HEAD_DIM
    return pl.pallas_call(
        _prep_kernel,
        grid=(b, s // rows),
        in_specs=[
            pl.BlockSpec((1, rows, GROUP_W), lambda bb, r: (bb, r, OFF_AK // GROUP_W)),
            pl.BlockSpec((1, rows, GROUP_W), lambda bb, r: (bb, r, OFF_BQ // GROUP_W)),
            pl.BlockSpec((1, rows, kwb), lambda bb, r: (bb, r, OFF_BK // kwb)),
            pl.BlockSpec((1, rows, kwc), lambda bb, r: (bb, r, OFF_CK // kwc)),
            pl.BlockSpec((1, rows, GROUP_W), lambda bb, r: (bb, r, OFF_DK // GROUP_W)),
            pl.BlockSpec((rows, HEAD_DIM), lambda bb, r: (r, 0)),
            pl.BlockSpec((rows, HEAD_DIM), lambda bb, r: (r, 0)),
            pl.BlockSpec((1, HEAD_DIM), lambda bb, r: (0, 0)),
            pl.BlockSpec((1, HEAD_DIM), lambda bb, r: (0, 0)),
        ],
        out_specs=[pl.BlockSpec((1, rows, GROUP_W), lambda bb, r: (bb, r, 0)),
                   pl.BlockSpec((1, KT_HEADS, HEAD_DIM, rows), lambda bb, r: (bb, 0, 0, r))],
        out_shape=[jax.ShapeDtypeStruct((b, s, GROUP_W), BF16),
                   jax.ShapeDtypeStruct((b, KT_HEADS, HEAD_DIM, s), BF16)],
        compiler_params=_cparams(("parallel", "parallel")),
        name="prep",
    )(proj, proj, proj, proj, proj, cos, sin_signed, q_gain.reshape(1, -1), k_gain.reshape(1, -1))


def _attn_a_kernel(q_ref, kt_ref, v_ref, tb_ref, o_ref, *, n_blocks, q_blocks):
    span = NA_SPAN_BLOCKS * Q_BLOCK
    grid_rows = 2 * n_blocks
    key_half = jnp.where(lax.broadcasted_iota(jnp.int32, (1, Q_BLOCK), 1) >= GRID_W, 1, 0)
    for c in range(q_blocks):
        i = pl.program_id(2) * q_blocks + c
        sb = jnp.clip(i - 2, 0, n_blocks - NA_SPAN_BLOCKS)
        start = pl.multiple_of(sb * Q_BLOCK, Q_BLOCK)
        rows = slice(c * Q_BLOCK, (c + 1) * Q_BLOCK)
        s = _dot(q_ref[0, rows, :], kt_ref[0, 0, :, pl.ds(start, span)])
        pieces = []
        for j in range(NA_SPAN_BLOCKS):
            blk = tb_ref[0, sb + j - i + NA_REL_BLOCKS // 2]
            key_row = 2 * (sb + j) + key_half
            halves = []
            for a in range(2):
                first = jnp.clip(2 * i + a - NA_WIN_ROWS // 2, 0, grid_rows - NA_WIN_ROWS)
                row_ok = (key_row >= first) & (key_row < first + NA_WIN_ROWS)
                halves.append(blk[a * GRID_W:(a + 1) * GRID_W] + jnp.where(row_ok, 0.0, NEG_INF))
            pieces.append(jnp.concatenate(halves, axis=0))
        e, l = _softmax_parts(s + jnp.concatenate(pieces, axis=1))
        o = _dot(e.astype(BF16), v_ref[0, pl.ds(start, span), :]) * (1.0 / l)
        o_ref[0, rows, :] = o.astype(o_ref.dtype)


def _attn_a(proj, kt, table, q_blocks=4):
    b, s, _ = proj.shape
    nb = s // Q_BLOCK
    hd = HEAD_DIM
    tq = q_blocks * Q_BLOCK
    return pl.pallas_call(
        functools.partial(_attn_a_kernel, n_blocks=nb, q_blocks=q_blocks),
        grid=(NA_HEADS, b, nb // q_blocks),
        in_specs=[
            pl.BlockSpec((1, tq, hd), lambda h, bb, i: (bb, i, OFF_AQ // hd + h)),
            pl.BlockSpec((1, 1, hd, s), lambda h, bb, i: (bb, KT_A + h, 0, 0)),
            pl.BlockSpec((1, s, hd), lambda h, bb, i: (bb, 0, OFF_AV // hd + h)),
            pl.BlockSpec((1, NA_REL_BLOCKS, Q_BLOCK, Q_BLOCK), lambda h, bb, i: (h, 0, 0, 0)),
        ],
        out_specs=pl.BlockSpec((1, tq, hd), lambda h, bb, i: (bb, i, h)),
        out_shape=jax.ShapeDtypeStruct((b, s, GROUP_W), BF16),
        compiler_params=_cparams(("parallel", "parallel", "arbitrary")),
        name="attn_a",
    )(proj, kt, proj, table)


def _attn_b_kernel(q_ref, kt_ref, v_ref, o_ref):
    kt = kt_ref[0, 0]
    v = v_ref[0]
    for g in range(AX_HEADS // AX_KV_HEADS):
        cols = slice(g * HEAD_DIM, (g + 1) * HEAD_DIM)
        e, l = _softmax_parts(_dot(q_ref[0, :, cols], kt))
        o = _dot(e.astype(BF16), v) * (1.0 / l)
        o_ref[0, :, cols] = o.astype(o_ref.dtype)


def _attn_b(qn, kt, proj, tq=256):
    b, s, _ = proj.shape
    qw = (AX_HEADS // AX_KV_HEADS) * HEAD_DIM
    return pl.pallas_call(
        _attn_b_kernel,
        grid=(b, AX_KV_HEADS, s // tq),
        in_specs=[
            pl.BlockSpec((1, tq, qw), lambda bb, kv, i: (bb, i, kv)),
            pl.BlockSpec((1, 1, HEAD_DIM, s), lambda bb, kv, i: (bb, KT_B + kv, 0, 0)),
            pl.BlockSpec((1, s, HEAD_DIM), lambda bb, kv, i: (bb, 0, OFF_BV // HEAD_DIM + kv)),
        ],
        out_specs=pl.BlockSpec((1, tq, qw), lambda bb, kv, i: (bb, i, kv)),
        out_shape=jax.ShapeDtypeStruct((b, s, GROUP_W), BF16),
        compiler_params=_cparams(("parallel", "parallel", "arbitrary")),
        name="attn_b",
    )(qn, kt, proj)


def _attn_c_kernel(q_ref, ktp_ref, ktc_ref, ktn_ref, vp_ref, vc_ref, vn_ref, t5_ref, sink_ref,
                   o_ref, *, n_blocks):
    kv = pl.program_id(1)
    i = pl.program_id(2)
    groups = SW_HEADS // SW_KV_HEADS
    kinds = (jnp.where(i > 0, T5_BAND0, T5_MASKED), T5_BAND0 + 1,
             jnp.where(i < n_blocks - 1, T5_BAND0 + 2, T5_MASKED))
    q = jnp.concatenate(
        [q_ref[0, :, g * HEAD_DIM:(g + 1) * HEAD_DIM] for g in range(groups)], axis=0)
    s = jnp.concatenate(
        [_dot(q, kt_ref[0, 0]) + jnp.concatenate([t5_ref[g, kind] for g in range(groups)], axis=0)
         for kt_ref, kind in zip((ktp_ref, ktc_ref, ktn_ref), kinds)], axis=1)
    sink = jnp.concatenate(
        [jnp.full((Q_BLOCK, 1), sink_ref[kv * groups + g] * LOG2E, F32) for g in range(groups)],
        axis=0)
    e, l = _softmax_parts(s, extra=sink)
    v = jnp.concatenate([vp_ref[0], vc_ref[0], vn_ref[0]], axis=0)
    o = _dot(e.astype(BF16), v) * (1.0 / l)
    for g in range(groups):
        o_ref[0, :, g * HEAD_DIM:(g + 1) * HEAD_DIM] = (
            o[g * Q_BLOCK:(g + 1) * Q_BLOCK].astype(o_ref.dtype))


def _attn_c(proj, kt, t5_blocks, sink):
    b, s, _ = proj.shape
    nb = s // Q_BLOCK
    groups = SW_HEADS // SW_KV_HEADS
    qw = groups * HEAD_DIM
    hd = HEAD_DIM

    def prev(i):
        return jnp.maximum(i - 1, 0)

    def nxt(i):
        return jnp.minimum(i + 1, nb - 1)

    kt_specs = [pl.BlockSpec((1, 1, hd, Q_BLOCK), lambda bb, kv, i, f=f: (bb, KT_C + kv, 0, f(i)))
                for f in (prev, lambda i: i, nxt)]
    v_specs = [pl.BlockSpec((1, Q_BLOCK, hd), lambda bb, kv, i, f=f: (bb, f(i), OFF_CV // hd + kv))
               for f in (prev, lambda i: i, nxt)]
    return pl.pallas_call(
        functools.partial(_attn_c_kernel, n_blocks=nb),
        grid=(b, SW_KV_HEADS, nb),
        in_specs=[pl.BlockSpec((1, Q_BLOCK, qw), lambda bb, kv, i: (bb, i, OFF_CQ // qw + kv))]
        + kt_specs + v_specs
        + [pl.BlockSpec((groups, T5_KINDS, Q_BLOCK, Q_BLOCK), lambda bb, kv, i: (kv, 0, 0, 0)),
           pl.BlockSpec(memory_space=pltpu.SMEM)],
        out_specs=pl.BlockSpec((1, Q_BLOCK, qw), lambda bb, kv, i: (bb, i, kv)),
        out_shape=jax.ShapeDtypeStruct((b, s, GROUP_W), BF16),
        compiler_params=_cparams(("parallel", "parallel", "arbitrary")),
        name="attn_c",
    )(proj, kt, kt, kt, proj, proj, proj, t5_blocks, sink)


def _attn_d_kernel(q1_ref, q2_ref, k1t_ref, k2t_ref, v_ref, t5_ref, lam_ref, g_ref, o_ref,
                   *, n_blocks, q_blocks, lambda_init):
    lp = lam_ref[...]
    lam = (jnp.exp(jnp.sum(lp[0:1] * lp[1:2], axis=-1, keepdims=True))
           - jnp.exp(jnp.sum(lp[2:3] * lp[3:4], axis=-1, keepdims=True)) + lambda_init)
    k1t = k1t_ref[0, 0]
    k2t = k2t_ref[0, 0]
    v = v_ref[0]
    for c in range(q_blocks):
        i = pl.program_id(2) * q_blocks + c
        rows = slice(c * Q_BLOCK, (c + 1) * Q_BLOCK)
        bias = jnp.concatenate(
            [t5_ref[0, jnp.clip(kj - i + T5_BAND0 + 1, T5_FAR_LO, T5_FAR_HI)]
             for kj in range(n_blocks)], axis=1)
        e1, l1 = _softmax_parts(_dot(q1_ref[0, rows, :], k1t) + bias)
        e2, l2 = _softmax_parts(_dot(q2_ref[0, rows, :], k2t) + bias)
        ev = _dot(jnp.concatenate([e1.astype(BF16), e2.astype(BF16)], axis=0), v)
        od = ev[:Q_BLOCK] * (1.0 / l1) - ev[Q_BLOCK:] * (lam / l2)
        o_ref[0, rows, :] = (_rms(od, g_ref[...]) * (1.0 - lambda_init)).astype(o_ref.dtype)


def _attn_d(proj, kt, t5_blocks, lam_params, subln, lambda_init, q_blocks=4):
    b, s, _ = proj.shape
    nb = s // Q_BLOCK
    hd = HEAD_DIM
    tq = q_blocks * Q_BLOCK
    return pl.pallas_call(
        functools.partial(_attn_d_kernel, n_blocks=nb, q_blocks=q_blocks, lambda_init=lambda_init),
        grid=(b, DF_HEADS, nb // q_blocks),
        in_specs=[
            pl.BlockSpec((1, tq, hd), lambda bb, h, i: (bb, i, OFF_DQ // hd + h)),
            pl.BlockSpec((1, tq, hd), lambda bb, h, i: (bb, i, OFF_DQ // hd + DF_HEADS + h)),
            pl.BlockSpec((1, 1, hd, s), lambda bb, h, i: (bb, KT_D + h, 0, 0)),
            pl.BlockSpec((1, 1, hd, s), lambda bb, h, i: (bb, KT_D + DF_HEADS + h, 0, 0)),
            pl.BlockSpec((1, s, DF_V_DIM), lambda bb, h, i: (bb, 0, OFF_DV // DF_V_DIM + h)),
            pl.BlockSpec((1, T5_KINDS, Q_BLOCK, Q_BLOCK), lambda bb, h, i: (SW_HEADS + h, 0, 0, 0)),
            pl.BlockSpec((4, hd), lambda bb, h, i: (0, 0)),
            pl.BlockSpec((1, DF_V_DIM), lambda bb, h, i: (0, 0)),
        ],
        out_specs=pl.BlockSpec((1, tq, DF_V_DIM), lambda bb, h, i: (bb, i, h)),
        out_shape=jax.ShapeDtypeStruct((b, s, GROUP_W), BF16),
        compiler_params=_cparams(("parallel", "parallel", "arbitrary")),
        name="attn_d",
    )(proj, proj, kt, kt, proj, t5_blocks, lam_params, subln.reshape(1, -1))


def _rope_tables(s):
    pos = jnp.arange(s, dtype=jnp.int32)
    row = (pos // GRID_W).astype(F32)
    col = (pos % GRID_W).astype(F32)
    n_pairs = HEAD_DIM // 4
    inv = ROPE_THETA ** (-jnp.arange(n_pairs, dtype=F32) / n_pairs)
    ang = jnp.concatenate([row[:, None] * inv, col[:, None] * inv], axis=-1)
    cos = jnp.repeat(jnp.cos(ang), 2, axis=-1)
    sin = jnp.repeat(jnp.sin(ang), 2, axis=-1)
    sign = jnp.tile(jnp.asarray([-1.0, 1.0], F32), HEAD_DIM // 2)
    return cos, sin * sign


def _q_col_scale():
    col = jnp.arange(D_IN, dtype=jnp.int32)
    is_q = (((col >= OFF_AQ) & (col < OFF_AK)) | ((col >= OFF_CQ) & (col < OFF_CK))
            | ((col >= OFF_DQ) & (col < OFF_DK)))
    return jnp.where(is_q, LOGIT_SCALE, 1.0).astype(F32).reshape(1, D_IN)


def kernel(x, ln_attn_pre, ln_attn_post, ln_mlp_pre, ln_mlp_post, w_in, w_out, na_rpb,
           ax_q_norm, ax_k_norm, sw_sink, df_lambda, df_subln, t5_table, w_mlp_in, w_mlp_out):
    b, s, d = x.shape
    depth = w_in.shape[0]
    t = b * s
    cos, sin_signed = _rope_tables(s)
    col_scale = _q_col_scale()
    t5_blocks = _t5_blocks(t5_table)
    xf = x.reshape(t, d)
    h = _norm_cast(xf, ln_attn_pre[0])
    for l in range(depth):
        proj = _matmul(h, w_in, l, BF16, col_scale=col_scale, name="in_proj").reshape(b, s, D_IN)
        qn_b, kt = _prep(proj, cos, sin_signed, ax_q_norm[l], ax_k_norm[l])
        ya = _attn_a(proj, kt, _na_table(na_rpb[l]))
        yb = _attn_b(qn_b, kt, proj)
        yc = _attn_c(proj, kt, t5_blocks, sw_sink[l])
        lambda_init = 0.8 - 0.6 * math.exp(-0.3 * l)
        yd = _attn_d(proj, kt, t5_blocks, df_lambda[l], df_subln[l], lambda_init)
        y = _out_proj(ya.reshape(t, -1), yb.reshape(t, -1), yc.reshape(t, -1), yd.reshape(t, -1),
                      w_out, l)
        xf, h = _resid_norm(xf, y, ln_attn_post[l], ln_mlp_pre[l])
        u = _matmul(h, w_mlp_in, l, BF16, act=_relu2, name="mlp_in")
        y = _matmul(u, w_mlp_out, l, F32, tn=1024, tk=2048, name="mlp_out")
        g_next = ln_attn_pre[l + 1] if l + 1 < depth else None
        xf, h = _resid_norm(xf, y, ln_mlp_post[l], g_next)
    return xf.reshape(b, s, d)
```

```python
import functools
import math

import jax
import jax.numpy as jnp
from jax import lax
from jax.experimental import pallas as pl
from jax.experimental.pallas import tpu as pltpu

F32 = jnp.float32
BF16 = jnp.bfloat16

D_MODEL = 4096
GRID_W = 64
HEAD_DIM = 128
Q_BLOCK = 128
GROUP_W = D_MODEL // 4
NA_HEADS = GROUP_W // HEAD_DIM
NA_WIN_ROWS = 8
NA_WIN_COLS = 16
AX_HEADS = GROUP_W // HEAD_DIM
AX_KV_HEADS = AX_HEADS // 4
ROPE_THETA = 10000.0
SW_HEADS = GROUP_W // HEAD_DIM
SW_KV_HEADS = SW_HEADS // 4
SW_WINDOW = 128
DF_V_DIM = 2 * HEAD_DIM
DF_HEADS = GROUP_W // DF_V_DIM
T5_BUCKETS = 32
T5_MAX_DIST = 128
T5_HEADS = SW_HEADS + DF_HEADS
EPS = 1e-6
SCALE = HEAD_DIM ** -0.5
LOG2E = math.log2(math.e)
LOGIT_SCALE = SCALE * LOG2E

OFF_AQ, OFF_AK, OFF_AV = 0, 1024, 2048
OFF_BQ, OFF_BK, OFF_BV = 3072, 4096, 4352
OFF_CQ, OFF_CK, OFF_CV = 4608, 5632, 5888
OFF_DQ, OFF_DK, OFF_DV = 6144, 7168, 8192
D_IN = 9216

KT_A, KT_B, KT_C, KT_D = 0, NA_HEADS, NA_HEADS + AX_KV_HEADS, NA_HEADS + AX_KV_HEADS + SW_KV_HEADS
KT_HEADS = KT_D + 2 * DF_HEADS

NA_SPAN_BLOCKS = 5
NA_REL_BLOCKS = 9
NA_RPB_ROWS = 2 * NA_WIN_ROWS - 1
NA_RPB_COLS = 2 * NA_WIN_COLS - 1

T5_FAR_LO, T5_BAND0, T5_FAR_HI, T5_MASKED, T5_KINDS = 0, 1, 4, 5, 6

VMEM_LIMIT = 60 * 1024 * 1024
NEG_INF = float("-inf")


def _cparams(sem):
    return pltpu.CompilerParams(dimension_semantics=sem, vmem_limit_bytes=VMEM_LIMIT)


def _dot(a, b):
    return jnp.dot(a, b, preferred_element_type=F32)


def _rms(x, g):
    return x * lax.rsqrt(jnp.mean(x * x, axis=-1, keepdims=True) + EPS) * g


def _softmax_parts(s, extra=None):
    m = jnp.max(s, axis=-1, keepdims=True)
    if extra is not None:
        m = jnp.maximum(m, extra)
    e = jnp.exp2(s - m)
    l = jnp.sum(e, axis=-1, keepdims=True)
    if extra is not None:
        l = l + jnp.exp2(extra - m)
    return e, l


def _norm_cast_kernel(x_ref, g_ref, h_ref):
    h_ref[...] = _rms(x_ref[...], g_ref[...]).astype(h_ref.dtype)


def _norm_cast(x, g, rows=256):
    t, d = x.shape
    return pl.pallas_call(
        _norm_cast_kernel,
        grid=(t // rows,),
        in_specs=[pl.BlockSpec((rows, d), lambda i: (i, 0)),
                  pl.BlockSpec((1, d), lambda i: (0, 0))],
        out_specs=pl.BlockSpec((rows, d), lambda i: (i, 0)),
        out_shape=jax.ShapeDtypeStruct((t, d), BF16),
        compiler_params=_cparams(("parallel",)),
        name="norm_cast",
    )(x, g.reshape(1, d))


def _resid_norm_kernel(x_ref, y_ref, gp_ref, gn_ref, xo_ref, h_ref):
    xn = x_ref[...] + _rms(y_ref[...], gp_ref[...])
    xo_ref[...] = xn
    h_ref[...] = _rms(xn, gn_ref[...]).astype(h_ref.dtype)


def _resid_kernel(x_ref, y_ref, gp_ref, xo_ref):
    xo_ref[...] = x_ref[...] + _rms(y_ref[...], gp_ref[...])


def _resid_norm(x, y, g_post, g_next, rows=256):
    t, d = x.shape
    row_spec = pl.BlockSpec((rows, d), lambda i: (i, 0))
    g_spec = pl.BlockSpec((1, d), lambda i: (0, 0))
    if g_next is None:
        return pl.pallas_call(
            _resid_kernel, grid=(t // rows,),
            in_specs=[row_spec, row_spec, g_spec], out_specs=row_spec,
            out_shape=jax.ShapeDtypeStruct((t, d), F32),
            compiler_params=_cparams(("parallel",)), name="resid",
        )(x, y, g_post.reshape(1, d)), None
    return pl.pallas_call(
        _resid_norm_kernel, grid=(t // rows,),
        in_specs=[row_spec, row_spec, g_spec, g_spec], out_specs=[row_spec, row_spec],
        out_shape=[jax.ShapeDtypeStruct((t, d), F32), jax.ShapeDtypeStruct((t, d), BF16)],
        compiler_params=_cparams(("parallel",)), name="resid_norm",
    )(x, y, g_post.reshape(1, d), g_next.reshape(1, d))


def _relu2(r):
    r = jnp.maximum(r, 0.0)
    return r * r


def _mm_kernel(a_ref, w_ref, *rest, act, col_scaled):
    o_ref = rest[-1]
    r = _dot(a_ref[...], w_ref[...].astype(BF16))
    if act is not None:
        r = act(r)
    if col_scaled:
        r = r * rest[0][...]
    o_ref[...] = r.astype(o_ref.dtype)


def _matmul(a, w, layer, out_dtype, *, tm=2048, tn=512, act=None, col_scale=None, name="matmul"):
    m, kdim = a.shape
    _, _, n = w.shape
    in_specs = [pl.BlockSpec((tm, kdim), lambda i, j: (i, 0)),
                pl.BlockSpec((None, kdim, tn), lambda i, j: (layer, 0, j))]
    args = [a, w]
    if col_scale is not None:
        in_specs.append(pl.BlockSpec((1, tn), lambda i, j: (0, j)))
        args.append(col_scale)
    return pl.pallas_call(
        functools.partial(_mm_kernel, act=act, col_scaled=col_scale is not None),
        grid=(m // tm, n // tn),
        in_specs=in_specs,
        out_specs=pl.BlockSpec((tm, tn), lambda i, j: (i, j)),
        out_shape=jax.ShapeDtypeStruct((m, n), out_dtype),
        compiler_params=_cparams(("parallel", "parallel")),
        name=name,
    )(*args)


def _mm_ksplit_kernel(a_ref, w_ref, o_ref):
    @pl.when(pl.program_id(2) == 0)
    def _():
        o_ref[...] = jnp.zeros(o_ref.shape, o_ref.dtype)

    o_ref[...] += _dot(a_ref[...], w_ref[...].astype(BF16))


def _matmul_ksplit(a, w, layer, *, tm=2048, tn=1024, tk=2048, name="matmul_ksplit"):
    m, kdim = a.shape
    _, _, n = w.shape
    return pl.pallas_call(
        _mm_ksplit_kernel,
        grid=(m // tm, n // tn, kdim // tk),
        in_specs=[pl.BlockSpec((tm, tk), lambda i, j, k: (i, k)),
                  pl.BlockSpec((None, tk, tn), lambda i, j, k: (layer, k, j))],
        out_specs=pl.BlockSpec((tm, tn), lambda i, j, k: (i, j)),
        out_shape=jax.ShapeDtypeStruct((m, n), F32),
        compiler_params=_cparams(("parallel", "parallel", "arbitrary")),
        name=name,
    )(a, w)


def _out_proj_kernel(ya_ref, yb_ref, yc_ref, yd_ref, w_ref, o_ref):
    acc = _dot(ya_ref[...], w_ref[0:GROUP_W, :].astype(BF16))
    acc += _dot(yb_ref[...], w_ref[GROUP_W:2 * GROUP_W, :].astype(BF16))
    acc += _dot(yc_ref[...], w_ref[2 * GROUP_W:3 * GROUP_W, :].astype(BF16))
    acc += _dot(yd_ref[...], w_ref[3 * GROUP_W:4 * GROUP_W, :].astype(BF16))
    o_ref[...] = acc


def _out_proj(ya, yb, yc, yd, w, layer, tm=2048, tn=256):
    m = ya.shape[0]
    _, kdim, n = w.shape
    a_spec = pl.BlockSpec((tm, GROUP_W), lambda i, j: (i, 0))
    return pl.pallas_call(
        _out_proj_kernel,
        grid=(m // tm, n // tn),
        in_specs=[a_spec, a_spec, a_spec, a_spec,
                  pl.BlockSpec((None, kdim, tn), lambda i, j: (layer, 0, j))],
        out_specs=pl.BlockSpec((tm, tn), lambda i, j: (i, j)),
        out_shape=jax.ShapeDtypeStruct((m, n), F32),
        compiler_params=_cparams(("parallel", "parallel")),
        name="out_proj",
    )(ya, yb, yc, yd, w)


def _t5_bucket(rel):
    nb = T5_BUCKETS // 2
    max_exact = nb // 2
    base = jnp.where(rel > 0, nb, 0)
    n = jnp.abs(rel)
    n_f = jnp.maximum(n, 1).astype(F32)
    large = max_exact + (jnp.log(n_f / max_exact) / math.log(T5_MAX_DIST / max_exact)
                         * (nb - max_exact)).astype(jnp.int32)
    large = jnp.minimum(large, nb - 1)
    return base + jnp.where(n < max_exact, n, large)


def _t5_blocks_kernel(idx_ref, tab_ref, o_ref):
    h = pl.program_id(0)
    kind = pl.program_id(1)
    idx = idx_ref[0]
    band = jnp.zeros(idx.shape, F32)
    for b in range(T5_BUCKETS):
        band = jnp.where(idx == b, tab_ref[b, h], band)
    t = lax.broadcasted_iota(jnp.int32, idx.shape, 0)
    u = lax.broadcasted_iota(jnp.int32, idx.shape, 1)
    rel = u + (kind - (T5_BAND0 + 1)) * Q_BLOCK - t
    allowed = (jnp.abs(rel) <= SW_WINDOW) | (h >= SW_HEADS)
    val = jnp.where(allowed, band, NEG_INF)
    val = jnp.where(kind == T5_FAR_LO, tab_ref[T5_BUCKETS // 2 - 1, h], val)
    val = jnp.where(kind == T5_FAR_HI, tab_ref[T5_BUCKETS - 1, h], val)
    val = jnp.where(kind == T5_MASKED, NEG_INF, val)
    o_ref[0, 0] = val * LOG2E


def _t5_blocks(t5_table):
    t = jnp.arange(Q_BLOCK, dtype=jnp.int32)
    rel = (jnp.arange(3 * Q_BLOCK, dtype=jnp.int32)[None, :] - Q_BLOCK - t[:, None])
    idx = _t5_bucket(rel).astype(jnp.int32).reshape(Q_BLOCK, 3, Q_BLOCK).transpose(1, 0, 2)
    return pl.pallas_call(
        _t5_blocks_kernel,
        grid=(T5_HEADS, T5_KINDS),
        in_specs=[pl.BlockSpec((1, Q_BLOCK, Q_BLOCK),
                               lambda h, kind: (jnp.clip(kind - T5_BAND0, 0, 2), 0, 0)),
                  pl.BlockSpec(memory_space=pltpu.SMEM)],
        out_specs=pl.BlockSpec((1, 1, Q_BLOCK, Q_BLOCK), lambda h, kind: (h, kind, 0, 0)),
        out_shape=jax.ShapeDtypeStruct((T5_HEADS, T5_KINDS, Q_BLOCK, Q_BLOCK), F32),
        compiler_params=_cparams(("arbitrary", "arbitrary")),
        name="t5_blocks",
    )(idx, t5_table)


def _na_table_kernel(rpb_ref, o_ref):
    h = pl.program_id(0)
    dblk = pl.program_id(1) - (NA_REL_BLOCKS // 2)
    t = lax.broadcasted_iota(jnp.int32, (Q_BLOCK, Q_BLOCK), 0)
    u = lax.broadcasted_iota(jnp.int32, (Q_BLOCK, Q_BLOCK), 1)
    q_hi = t >= GRID_W
    k_hi = u >= GRID_W
    c = jnp.bitwise_and(t, GRID_W - 1)
    kc = jnp.bitwise_and(u, GRID_W - 1)
    cs = jnp.clip(c - NA_WIN_COLS // 2, 0, GRID_W - NA_WIN_COLS)
    col_ok = (kc >= cs) & (kc < cs + NA_WIN_COLS)
    dc = kc - c + (NA_WIN_COLS - 1)
    neg = jnp.float32(NEG_INF)

    def entry(dr, d):
        ok = (dr >= -(NA_WIN_ROWS - 1)) & (dr <= NA_WIN_ROWS - 1)
        row = jnp.clip(dr + NA_WIN_ROWS - 1, 0, NA_RPB_ROWS - 1)
        v = rpb_ref[h * (NA_RPB_ROWS * NA_RPB_COLS) + row * NA_RPB_COLS + d]
        return jnp.where(ok, v, neg)

    val = jnp.full((Q_BLOCK, Q_BLOCK), neg, F32)
    for d in range(NA_RPB_COLS):
        v00 = entry(2 * dblk, d)
        v01 = entry(2 * dblk + 1, d)
        v10 = entry(2 * dblk - 1, d)
        v11 = entry(2 * dblk, d)
        vm = jnp.where(q_hi, jnp.where(k_hi, v11, v10), jnp.where(k_hi, v01, v00))
        val = jnp.where((dc == d) & col_ok, vm, val)
    o_ref[0, 0] = val * LOG2E


def _na_table(rpb):
    return pl.pallas_call(
        _na_table_kernel,
        grid=(NA_HEADS, NA_REL_BLOCKS),
        in_specs=[pl.BlockSpec(memory_space=pltpu.SMEM)],
        out_specs=pl.BlockSpec((1, 1, Q_BLOCK, Q_BLOCK), lambda h, d: (h, d, 0, 0)),
        out_shape=jax.ShapeDtypeStruct((NA_HEADS, NA_REL_BLOCKS, Q_BLOCK, Q_BLOCK), F32),
        compiler_params=_cparams(("arbitrary", "arbitrary")),
        name="na_table",
    )(rpb.reshape(-1))


def _rope(x, cos, sin_signed):
    lane = lax.broadcasted_iota(jnp.int32, x.shape, 1)
    even = jnp.bitwise_and(lane, 1) == 0
    swapped = jnp.where(even, pltpu.roll(x, HEAD_DIM - 1, 1), pltpu.roll(x, 1, 1))
    return x * cos + swapped * sin_signed


def _prep_kernel(ak_ref, bq_ref, bk_ref, ck_ref, dk_ref, cos_ref, sin_ref, qg_ref, kg_ref,
                 qo_ref, kt_ref):
    cos = cos_ref[...]
    sin = sin_ref[...]

    def head(ref, h):
        return ref[0, :, h * HEAD_DIM:(h + 1) * HEAD_DIM].astype(F32)

    for h in range(AX_HEADS):
        qh = _rope(_rms(head(bq_ref, h), qg_ref[...]), cos, sin)
        qo_ref[0, :, h * HEAD_DIM:(h + 1) * HEAD_DIM] = (qh * LOGIT_SCALE).astype(qo_ref.dtype)
    for h in range(NA_HEADS):
        kt_ref[0, KT_A + h] = head(ak_ref, h).T.astype(kt_ref.dtype)
    for h in range(AX_KV_HEADS):
        kh = _rope(_rms(head(bk_ref, h), kg_ref[...]), cos, sin)
        kt_ref[0, KT_B + h] = kh.T.astype(kt_ref.dtype)
    for h in range(SW_KV_HEADS):
        kt_ref[0, KT_C + h] = head(ck_ref, h).T.astype(kt_ref.dtype)
    for h in range(2 * DF_HEADS):
        kt_ref[0, KT_D + h] = head(dk_ref, h).T.astype(kt_ref.dtype)


def _prep(proj, cos, sin_signed, q_gain, k_gain, rows=512):
    b, s, _ = proj.shape
    kwb = AX_KV_HEADS * HEAD_DIM
    kwc = SW_KV_HEADS * HEAD_DIM
    return pl.pallas_call(
        _prep_kernel,
        grid=(b, s // rows),
        in_specs=[
            pl.BlockSpec((1, rows, GROUP_W), lambda bb, r: (bb, r, OFF_AK // GROUP_W)),
            pl.BlockSpec((1, rows, GROUP_W), lambda bb, r: (bb, r, OFF_BQ // GROUP_W)),
            pl.BlockSpec((1, rows, kwb), lambda bb, r: (bb, r, OFF_BK // kwb)),
            pl.BlockSpec((1, rows, kwc), lambda bb, r: (bb, r, OFF_CK // kwc)),
            pl.BlockSpec((1, rows, GROUP_W), lambda bb, r: (bb, r, OFF_DK // GROUP_W)),
            pl.BlockSpec((rows, HEAD_DIM), lambda bb, r: (r, 0)),
            pl.BlockSpec((rows, HEAD_DIM), lambda bb, r: (r, 0)),
            pl.BlockSpec((1, HEAD_DIM), lambda bb, r: (0, 0)),
            pl.BlockSpec((1, HEAD_DIM), lambda bb, r: (0, 0)),
        ],
        out_specs=[pl.BlockSpec((1, rows, GROUP_W), lambda bb, r: (bb, r, 0)),
                   pl.BlockSpec((1, KT_HEADS, HEAD_DIM, rows), lambda bb, r: (bb, 0, 0, r))],
        out_shape=[jax.ShapeDtypeStruct((b, s, GROUP_W), BF16),
                   jax.ShapeDtypeStruct((b, KT_HEADS, HEAD_DIM, s), BF16)],
        compiler_params=_cparams(("parallel", "parallel")),
        name="prep",
    )(proj, proj, proj, proj, proj, cos, sin_signed, q_gain.reshape(1, -1), k_gain.reshape(1, -1))


def _attn_a_kernel(q_ref, kt_ref, v_ref, tb_ref, o_ref, *, n_blocks, q_blocks):
    span = NA_SPAN_BLOCKS * Q_BLOCK
    grid_rows = 2 * n_blocks
    key_half = jnp.where(lax.broadcasted_iota(jnp.int32, (1, Q_BLOCK), 1) >= GRID_W, 1, 0)
    for c in range(q_blocks):
        i = pl.program_id(2) * q_blocks + c
        sb = jnp.clip(i - 2, 0, n_blocks - NA_SPAN_BLOCKS)
        start = pl.multiple_of(sb * Q_BLOCK, Q_BLOCK)
        rows = slice(c * Q_BLOCK, (c + 1) * Q_BLOCK)
        s = _dot(q_ref[0, rows, :], kt_ref[0, 0, :, pl.ds(start, span)])
        pieces = []
        for j in range(NA_SPAN_BLOCKS):
            blk = tb_ref[0, sb + j - i + NA_REL_BLOCKS // 2]
            key_row = 2 * (sb + j) + key_half
            halves = []
            for a in range(2):
                first = jnp.clip(2 * i + a - NA_WIN_ROWS // 2, 0, grid_rows - NA_WIN_ROWS)
                row_ok = (key_row >= first) & (key_row < first + NA_WIN_ROWS)
                halves.append(blk[a * GRID_W:(a + 1) * GRID_W] + jnp.where(row_ok, 0.0, NEG_INF))
            pieces.append(jnp.concatenate(halves, axis=0))
        e, l = _softmax_parts(s + jnp.concatenate(pieces, axis=1))
        o = _dot(e.astype(BF16), v_ref[0, pl.ds(start, span), :]) * (1.0 / l)
        o_ref[0, rows, :] = o.astype(o_ref.dtype)


def _attn_a(proj, kt, table, q_blocks=4):
    b, s, _ = proj.shape
    nb = s // Q_BLOCK
    hd = HEAD_DIM
    tq = q_blocks * Q_BLOCK
    return pl.pallas_call(
        functools.partial(_attn_a_kernel, n_blocks=nb, q_blocks=q_blocks),
        grid=(NA_HEADS, b, nb // q_blocks),
        in_specs=[
            pl.BlockSpec((1, tq, hd), lambda h, bb, i: (bb, i, OFF_AQ // hd + h)),
            pl.BlockSpec((1, 1, hd, s), lambda h, bb, i: (bb, KT_A + h, 0, 0)),
            pl.BlockSpec((1, s, hd), lambda h, bb, i: (bb, 0, OFF_AV // hd + h)),
            pl.BlockSpec((1, NA_REL_BLOCKS, Q_BLOCK, Q_BLOCK), lambda h, bb, i: (h, 0, 0, 0)),
        ],
        out_specs=pl.BlockSpec((1, tq, hd), lambda h, bb, i: (bb, i, h)),
        out_shape=jax.ShapeDtypeStruct((b, s, GROUP_W), BF16),
        compiler_params=_cparams(("parallel", "parallel", "arbitrary")),
        name="attn_a",
    )(proj, kt, proj, table)


def _attn_b_kernel(q_ref, kt_ref, v_ref, o_ref):
    kt = kt_ref[0, 0]
    v = v_ref[0]
    for g in range(AX_HEADS // AX_KV_HEADS):
        cols = slice(g * HEAD_DIM, (g + 1) * HEAD_DIM)
        e, l = _softmax_parts(_dot(q_ref[0, :, cols], kt))
        o = _dot(e.astype(BF16), v) * (1.0 / l)
        o_ref[0, :, cols] = o.astype(o_ref.dtype)


def _attn_b(qn, kt, proj, tq=256):
    b, s, _ = proj.shape
    qw = (AX_HEADS // AX_KV_HEADS) * HEAD_DIM
    return pl.pallas_call(
        _attn_b_kernel,
        grid=(b, AX_KV_HEADS, s // tq),
        in_specs=[
            pl.BlockSpec((1, tq, qw), lambda bb, kv, i: (bb, i, kv)),
            pl.BlockSpec((1, 1, HEAD_DIM, s), lambda bb, kv, i: (bb, KT_B + kv, 0, 0)),
            pl.BlockSpec((1, s, HEAD_DIM), lambda bb, kv, i: (bb, 0, OFF_BV // HEAD_DIM + kv)),
        ],
        out_specs=pl.BlockSpec((1, tq, qw), lambda bb, kv, i: (bb, i, kv)),
        out_shape=jax.ShapeDtypeStruct((b, s, GROUP_W), BF16),
        compiler_params=_cparams(("parallel", "parallel", "arbitrary")),
        name="attn_b",
    )(qn, kt, proj)


def _attn_c_kernel(q_ref, ktp_ref, ktc_ref, ktn_ref, vp_ref, vc_ref, vn_ref, t5_ref, sink_ref,
                   o_ref, *, n_blocks):
    kv = pl.program_id(1)
    i = pl.program_id(2)
    groups = SW_HEADS // SW_KV_HEADS
    kinds = (jnp.where(i > 0, T5_BAND0, T5_MASKED), T5_BAND0 + 1,
             jnp.where(i < n_blocks - 1, T5_BAND0 + 2, T5_MASKED))
    q = jnp.concatenate(
        [q_ref[0, :, g * HEAD_DIM:(g + 1) * HEAD_DIM] for g in range(groups)], axis=0)
    s = jnp.concatenate(
        [_dot(q, kt_ref[0, 0]) + jnp.concatenate([t5_ref[g, kind] for g in range(groups)], axis=0)
         for kt_ref, kind in zip((ktp_ref, ktc_ref, ktn_ref), kinds)], axis=1)
    sink = jnp.concatenate(
        [jnp.full((Q_BLOCK, 1), sink_ref[kv * groups + g] * LOG2E, F32) for g in range(groups)],
        axis=0)
    e, l = _softmax_parts(s, extra=sink)
    v = jnp.concatenate([vp_ref[0], vc_ref[0], vn_ref[0]], axis=0)
    o = _dot(e.astype(BF16), v) * (1.0 / l)
    for g in range(groups):
        o_ref[0, :, g * HEAD_DIM:(g + 1) * HEAD_DIM] = (
            o[g * Q_BLOCK:(g + 1) * Q_BLOCK].astype(o_ref.dtype))


def _attn_c(proj, kt, t5_blocks, sink):
    b, s, _ = proj.shape
    nb = s // Q_BLOCK
    groups = SW_HEADS // SW_KV_HEADS
    qw = groups * HEAD_DIM
    hd = HEAD_DIM

    def prev(i):
        return jnp.maximum(i - 1, 0)

    def nxt(i):
        return jnp.minimum(i + 1, nb - 1)

    kt_specs = [pl.BlockSpec((1, 1, hd, Q_BLOCK), lambda bb, kv, i, f=f: (bb, KT_C + kv, 0, f(i)))
                for f in (prev, lambda i: i, nxt)]
    v_specs = [pl.BlockSpec((1, Q_BLOCK, hd), lambda bb, kv, i, f=f: (bb, f(i), OFF_CV // hd + kv))
               for f in (prev, lambda i: i, nxt)]
    return pl.pallas_call(
        functools.partial(_attn_c_kernel, n_blocks=nb),
        grid=(b, SW_KV_HEADS, nb),
        in_specs=[pl.BlockSpec((1, Q_BLOCK, qw), lambda bb, kv, i: (bb, i, OFF_CQ // qw + kv))]
        + kt_specs + v_specs
        + [pl.BlockSpec((groups, T5_KINDS, Q_BLOCK, Q_BLOCK), lambda bb, kv, i: (kv, 0, 0, 0)),
           pl.BlockSpec(memory_space=pltpu.SMEM)],
        out_specs=pl.BlockSpec((1, Q_BLOCK, qw), lambda bb, kv, i: (bb, i, kv)),
        out_shape=jax.ShapeDtypeStruct((b, s, GROUP_W), BF16),
        compiler_params=_cparams(("parallel", "parallel", "arbitrary")),
        name="attn_c",
    )(proj, kt, kt, kt, proj, proj, proj, t5_blocks, sink)


def _attn_d_kernel(q1_ref, q2_ref, k1t_ref, k2t_ref, v_ref, t5_ref, lam_ref, g_ref, o_ref,
                   *, n_blocks, q_blocks, lambda_init):
    lp = lam_ref[...]
    lam = (jnp.exp(jnp.sum(lp[0:1] * lp[1:2], axis=-1, keepdims=True))
           - jnp.exp(jnp.sum(lp[2:3] * lp[3:4], axis=-1, keepdims=True)) + lambda_init)
    k1t = k1t_ref[0, 0]
    k2t = k2t_ref[0, 0]
    v = v_ref[0]
    for c in range(q_blocks):
        i = pl.program_id(2) * q_blocks + c
        rows = slice(c * Q_BLOCK, (c + 1) * Q_BLOCK)
        bias = jnp.concatenate(
            [t5_ref[0, jnp.clip(kj - i + T5_BAND0 + 1, T5_FAR_LO, T5_FAR_HI)]
             for kj in range(n_blocks)], axis=1)
        e1, l1 = _softmax_parts(_dot(q1_ref[0, rows, :], k1t) + bias)
        e2, l2 = _softmax_parts(_dot(q2_ref[0, rows, :], k2t) + bias)
        ev = _dot(jnp.concatenate([e1.astype(BF16), e2.astype(BF16)], axis=0), v)
        od = ev[:Q_BLOCK] * (1.0 / l1) - ev[Q_BLOCK:] * (lam / l2)
        o_ref[0, rows, :] = (_rms(od, g_ref[...]) * (1.0 - lambda_init)).astype(o_ref.dtype)


def _attn_d(proj, kt, t5_blocks, lam_params, subln, lambda_init, q_blocks=4):
    b, s, _ = proj.shape
    nb = s // Q_BLOCK
    hd = HEAD_DIM
    tq = q_blocks * Q_BLOCK
    return pl.pallas_call(
        functools.partial(_attn_d_kernel, n_blocks=nb, q_blocks=q_blocks, lambda_init=lambda_init),
        grid=(b, DF_HEADS, nb // q_blocks),
        in_specs=[
            pl.BlockSpec((1, tq, hd), lambda bb, h, i: (bb, i, OFF_DQ // hd + h)),
            pl.BlockSpec((1, tq, hd), lambda bb, h, i: (bb, i, OFF_DQ // hd + DF_HEADS + h)),
            pl.BlockSpec((1, 1, hd, s), lambda bb, h, i: (bb, KT_D + h, 0, 0)),
            pl.BlockSpec((1, 1, hd, s), lambda bb, h, i: (bb, KT_D + DF_HEADS + h, 0, 0)),
            pl.BlockSpec((1, s, DF_V_DIM), lambda bb, h, i: (bb, 0, OFF_DV // DF_V_DIM + h)),
            pl.BlockSpec((1, T5_KINDS, Q_BLOCK, Q_BLOCK), lambda bb, h, i: (SW_HEADS + h, 0, 0, 0)),
            pl.BlockSpec((4, hd), lambda bb, h, i: (0, 0)),
            pl.BlockSpec((1, DF_V_DIM), lambda bb, h, i: (0, 0)),
        ],
        out_specs=pl.BlockSpec((1, tq, DF_V_DIM), lambda bb, h, i: (bb, i, h)),
        out_shape=jax.ShapeDtypeStruct((b, s, GROUP_W), BF16),
        compiler_params=_cparams(("parallel", "parallel", "arbitrary")),
        name="attn_d",
    )(proj, proj, kt, kt, proj, t5_blocks, lam_params, subln.reshape(1, -1))


def _rope_tables(s):
    pos = jnp.arange(s, dtype=jnp.int32)
    row = (pos // GRID_W).astype(F32)
    col = (pos % GRID_W).astype(F32)
    n_pairs = HEAD_DIM // 4
    inv = ROPE_THETA ** (-jnp.arange(n_pairs, dtype=F32) / n_pairs)
    ang = jnp.concatenate([row[:, None] * inv, col[:, None] * inv], axis=-1)
    cos = jnp.repeat(jnp.cos(ang), 2, axis=-1)
    sin = jnp.repeat(jnp.sin(ang), 2, axis=-1)
    sign = jnp.tile(jnp.asarray([-1.0, 1.0], F32), HEAD_DIM // 2)
    return cos, sin * sign


def _q_col_scale():
    col = jnp.arange(D_IN, dtype=jnp.int32)
    is_q = (((col >= OFF_AQ) & (col < OFF_AK)) | ((col >= OFF_CQ) & (col < OFF_CK))
            | ((col >= OFF_DQ) & (col < OFF_DK)))
    return jnp.where(is_q, LOGIT_SCALE, 1.0).astype(F32).reshape(1, D_IN)


def kernel(x, ln_attn_pre, ln_attn_post, ln_mlp_pre, ln_mlp_post, w_in, w_out, na_rpb,
           ax_q_norm, ax_k_norm, sw_sink, df_lambda, df_subln, t5_table, w_mlp_in, w_mlp_out):
    b, s, d = x.shape
    depth = w_in.shape[0]
    t = b * s
    cos, sin_signed = _rope_tables(s)
    col_scale = _q_col_scale()
    t5_blocks = _t5_blocks(t5_table)
    xf = x.reshape(t, d)
    h = _norm_cast(xf, ln_attn_pre[0])
    for l in range(depth):
        proj = _matmul(h, w_in, l, BF16, col_scale=col_scale, name="in_proj").reshape(b, s, D_IN)
        qn_b, kt = _prep(proj, cos, sin_signed, ax_q_norm[l], ax_k_norm[l])
        ya = _attn_a(proj, kt, _na_table(na_rpb[l]))
        yb = _attn_b(qn_b, kt, proj)
        yc = _attn_c(proj, kt, t5_blocks, sw_sink[l])
        lambda_init = 0.8 - 0.6 * math.exp(-0.3 * l)
        yd = _attn_d(proj, kt, t5_blocks, df_lambda[l], df_subln[l], lambda_init)
        y = _out_proj(ya.reshape(t, -1), yb.reshape(t, -1), yc.reshape(t, -1), yd.reshape(t, -1),
                      w_out, l)
        xf, h = _resid_norm(xf, y, ln_attn_post[l], ln_mlp_pre[l])
        u = _matmul(h, w_mlp_in, l, BF16, act=_relu2, name="mlp_in")
        y = _matmul_ksplit(u, w_mlp_out, l, name="mlp_out")
        g_next = ln_attn_pre[l + 1] if l + 1 < depth else None
        xf, h = _resid_norm(xf, y, ln_mlp_post[l], g_next)
    return xf.reshape(b, s, d)
```

```python
import functools
import math

import jax
import jax.numpy as jnp
from jax import lax
from jax.experimental import pallas as pl
from jax.experimental.pallas import tpu as pltpu

F32 = jnp.float32
BF16 = jnp.bfloat16

D_MODEL = 4096
GRID_W = 64
HEAD_DIM = 128
Q_BLOCK = 128
GROUP_W = D_MODEL // 4
NA_HEADS = GROUP_W // HEAD_DIM
NA_WIN_ROWS = 8
NA_WIN_COLS = 16
AX_HEADS = GROUP_W // HEAD_DIM
AX_KV_HEADS = AX_HEADS // 4
ROPE_THETA = 10000.0
SW_HEADS = GROUP_W // HEAD_DIM
SW_KV_HEADS = SW_HEADS // 4
SW_WINDOW = 128
DF_V_DIM = 2 * HEAD_DIM
DF_HEADS = GROUP_W // DF_V_DIM
T5_BUCKETS = 32
T5_MAX_DIST = 128
T5_HEADS = SW_HEADS + DF_HEADS
EPS = 1e-6
SCALE = HEAD_DIM ** -0.5
LOG2E = math.log2(math.e)
LOGIT_SCALE = SCALE * LOG2E

OFF_AQ, OFF_AK, OFF_AV = 0, 1024, 2048
OFF_BQ, OFF_BK, OFF_BV = 3072, 4096, 4352
OFF_CQ, OFF_CK, OFF_CV = 4608, 5632, 5888
OFF_DQ, OFF_DK, OFF_DV = 6144, 7168, 8192
D_IN = 9216
MIX_A, MIX_B, MIX_C, MIX_D = 0, GROUP_W, 2 * GROUP_W, 3 * GROUP_W

KT_A, KT_B, KT_C, KT_D = 0, NA_HEADS, NA_HEADS + AX_KV_HEADS, NA_HEADS + AX_KV_HEADS + SW_KV_HEADS
KT_HEADS = KT_D + 2 * DF_HEADS

NA_SPAN_BLOCKS = 5
NA_REL_BLOCKS = 9
NA_RPB_ROWS = 2 * NA_WIN_ROWS - 1
NA_RPB_COLS = 2 * NA_WIN_COLS - 1
NA_RPB_ROWS_PADDED = 16

T5_FAR_LO, T5_BAND0, T5_FAR_HI, T5_MASKED, T5_KINDS = 0, 1, 4, 5, 6

VMEM_LIMIT = 60 * 1024 * 1024
NEG_INF = float("-inf")


def _cparams(sem):
    return pltpu.CompilerParams(dimension_semantics=sem, vmem_limit_bytes=VMEM_LIMIT)


def _dot(a, b):
    return jnp.dot(a, b, preferred_element_type=F32)


def _rms(x, g):
    return x * lax.rsqrt(jnp.mean(x * x, axis=-1, keepdims=True) + EPS) * g


def _softmax_parts(s, extra=None):
    m = jnp.max(s, axis=-1, keepdims=True)
    if extra is not None:
        m = jnp.maximum(m, extra)
    e = jnp.exp2(s - m)
    l = jnp.sum(e, axis=-1, keepdims=True)
    if extra is not None:
        l = l + jnp.exp2(extra - m)
    return e, l


def _norm_cast_kernel(x_ref, g_ref, h_ref):
    h_ref[...] = _rms(x_ref[...], g_ref[...]).astype(h_ref.dtype)


def _norm_cast(x, g, rows=256):
    t, d = x.shape
    return pl.pallas_call(
        _norm_cast_kernel,
        grid=(t // rows,),
        in_specs=[pl.BlockSpec((rows, d), lambda i: (i, 0)),
                  pl.BlockSpec((1, d), lambda i: (0, 0))],
        out_specs=pl.BlockSpec((rows, d), lambda i: (i, 0)),
        out_shape=jax.ShapeDtypeStruct((t, d), BF16),
        compiler_params=_cparams(("parallel",)),
        name="norm_cast",
    )(x, g.reshape(1, d))


def _resid_norm_kernel(x_ref, y_ref, gp_ref, gn_ref, xo_ref, h_ref):
    xn = x_ref[...] + _rms(y_ref[...].astype(F32), gp_ref[...])
    xo_ref[...] = xn
    h_ref[...] = _rms(xn, gn_ref[...]).astype(h_ref.dtype)


def _resid_kernel(x_ref, y_ref, gp_ref, xo_ref):
    xo_ref[...] = x_ref[...] + _rms(y_ref[...].astype(F32), gp_ref[...])


def _resid_norm(x, y, g_post, g_next, rows=256):
    t, d = x.shape
    row_spec = pl.BlockSpec((rows, d), lambda i: (i, 0))
    g_spec = pl.BlockSpec((1, d), lambda i: (0, 0))
    if g_next is None:
        return pl.pallas_call(
            _resid_kernel, grid=(t // rows,),
            in_specs=[row_spec, row_spec, g_spec], out_specs=row_spec,
            out_shape=jax.ShapeDtypeStruct((t, d), F32),
            compiler_params=_cparams(("parallel",)), name="resid",
        )(x, y, g_post.reshape(1, d)), None
    return pl.pallas_call(
        _resid_norm_kernel, grid=(t // rows,),
        in_specs=[row_spec, row_spec, g_spec, g_spec], out_specs=[row_spec, row_spec],
        out_shape=[jax.ShapeDtypeStruct((t, d), F32), jax.ShapeDtypeStruct((t, d), BF16)],
        compiler_params=_cparams(("parallel",)), name="resid_norm",
    )(x, y, g_post.reshape(1, d), g_next.reshape(1, d))


def _relu2(r):
    r = jnp.maximum(r, 0.0)
    return r * r


def _mm_kernel(a_ref, w_ref, *rest, act, col_scaled):
    o_ref = rest[-1]
    r = _dot(a_ref[...], w_ref[...].astype(BF16))
    if act is not None:
        r = act(r)
    if col_scaled:
        r = r * rest[0][...]
    o_ref[...] = r.astype(o_ref.dtype)


def _matmul(a, w, layer, out_dtype, *, tm=2048, tn=512, act=None, col_scale=None, name="matmul"):
    m, kdim = a.shape
    _, _, n = w.shape
    in_specs = [pl.BlockSpec((tm, kdim), lambda i, j: (i, 0)),
                pl.BlockSpec((None, kdim, tn), lambda i, j: (layer, 0, j))]
    args = [a, w]
    if col_scale is not None:
        in_specs.append(pl.BlockSpec((1, tn), lambda i, j: (0, j)))
        args.append(col_scale)
    return pl.pallas_call(
        functools.partial(_mm_kernel, act=act, col_scaled=col_scale is not None),
        grid=(m // tm, n // tn),
        in_specs=in_specs,
        out_specs=pl.BlockSpec((tm, tn), lambda i, j: (i, j)),
        out_shape=jax.ShapeDtypeStruct((m, n), out_dtype),
        compiler_params=_cparams(("parallel", "parallel")),
        name=name,
    )(*args)


def _mm_ksplit_kernel(a_ref, w_ref, o_ref, acc_ref):
    k = pl.program_id(2)

    @pl.when(k == 0)
    def _():
        acc_ref[...] = jnp.zeros(acc_ref.shape, acc_ref.dtype)

    acc_ref[...] += _dot(a_ref[...], w_ref[...].astype(BF16))

    @pl.when(k == pl.num_programs(2) - 1)
    def _():
        o_ref[...] = acc_ref[...].astype(o_ref.dtype)


def _matmul_ksplit(a, w, layer, out_dtype, *, tm=2048, tn=1024, tk=2048, name="matmul_ksplit"):
    m, kdim = a.shape
    _, _, n = w.shape
    return pl.pallas_call(
        _mm_ksplit_kernel,
        grid=(m // tm, n // tn, kdim // tk),
        in_specs=[pl.BlockSpec((tm, tk), lambda i, j, k: (i, k)),
                  pl.BlockSpec((None, tk, tn), lambda i, j, k: (layer, k, j))],
        out_specs=pl.BlockSpec((tm, tn), lambda i, j, k: (i, j)),
        out_shape=jax.ShapeDtypeStruct((m, n), out_dtype),
        scratch_shapes=[pltpu.VMEM((tm, tn), F32)],
        compiler_params=_cparams(("parallel", "parallel", "arbitrary")),
        name=name,
    )(a, w)


def _t5_bucket(rel):
    nb = T5_BUCKETS // 2
    max_exact = nb // 2
    base = jnp.where(rel > 0, nb, 0)
    n = jnp.abs(rel)
    n_f = jnp.maximum(n, 1).astype(F32)
    large = max_exact + (jnp.log(n_f / max_exact) / math.log(T5_MAX_DIST / max_exact)
                         * (nb - max_exact)).astype(jnp.int32)
    large = jnp.minimum(large, nb - 1)
    return base + jnp.where(n < max_exact, n, large)


def _t5_blocks_kernel(idx_ref, tab_ref, o_ref):
    h = pl.program_id(0)
    shape = (Q_BLOCK, Q_BLOCK)
    t = lax.broadcasted_iota(jnp.int32, shape, 0)
    u = lax.broadcasted_iota(jnp.int32, shape, 1)
    o_ref[0, T5_FAR_LO] = jnp.full(shape, tab_ref[T5_BUCKETS // 2 - 1, h] * LOG2E, F32)
    o_ref[0, T5_FAR_HI] = jnp.full(shape, tab_ref[T5_BUCKETS - 1, h] * LOG2E, F32)
    o_ref[0, T5_MASKED] = jnp.full(shape, NEG_INF, F32)
    for j in range(3):
        idx = idx_ref[j]
        diag = jnp.zeros(idx.shape, F32)
        for b in range(T5_BUCKETS):
            diag = jnp.where(idx == b, tab_ref[b, h], diag)
        upper = pltpu.roll(jnp.broadcast_to(diag[0:1], shape), 0, 1, stride=1, stride_axis=0)
        lower = pltpu.roll(jnp.broadcast_to(diag[1:2], shape), 0, 1, stride=1, stride_axis=0)
        band = jnp.where(u >= t, upper, lower)
        rel = u + (j - 1) * Q_BLOCK - t
        allowed = (jnp.abs(rel) <= SW_WINDOW) | (h >= SW_HEADS)
        o_ref[0, T5_BAND0 + j] = jnp.where(allowed, band, NEG_INF) * LOG2E


def _t5_blocks(t5_table):
    k = jnp.arange(Q_BLOCK, dtype=jnp.int32)
    block_off = (jnp.arange(3, dtype=jnp.int32) - 1) * Q_BLOCK
    rel = block_off[:, None, None] + jnp.stack([k, k - Q_BLOCK])[None]
    idx = _t5_bucket(rel).astype(jnp.int32)
    return pl.pallas_call(
        _t5_blocks_kernel,
        grid=(T5_HEADS,),
        in_specs=[pl.BlockSpec((3, 2, Q_BLOCK), lambda h: (0, 0, 0)),
                  pl.BlockSpec(memory_space=pltpu.SMEM)],
        out_specs=pl.BlockSpec((1, T5_KINDS, Q_BLOCK, Q_BLOCK), lambda h: (h, 0, 0, 0)),
        out_shape=jax.ShapeDtypeStruct((T5_HEADS, T5_KINDS, Q_BLOCK, Q_BLOCK), F32),
        compiler_params=_cparams(("arbitrary",)),
        name="t5_blocks",
    )(idx, t5_table)


def _na_table_kernel(rpb_ref, o_ref):
    shape = (GRID_W, Q_BLOCK)
    c = lax.broadcasted_iota(jnp.int32, shape, 0)
    u = lax.broadcasted_iota(jnp.int32, shape, 1)
    k_hi = u >= GRID_W
    kc = jnp.bitwise_and(u, GRID_W - 1)
    cs = jnp.clip(c - NA_WIN_COLS // 2, 0, GRID_W - NA_WIN_COLS)
    col_ok = (kc >= cs) & (kc < cs + NA_WIN_COLS)

    def toeplitz(dr, lane_off):
        if abs(dr) > NA_WIN_ROWS - 1:
            return jnp.full(shape, NEG_INF, F32)
        row = dr + NA_WIN_ROWS - 1
        vec = jnp.broadcast_to(rpb_ref[0, row:row + 1, :], shape)
        shift = (lane_off - (NA_WIN_COLS - 1)) % Q_BLOCK
        return pltpu.roll(vec, shift, 1, stride=1, stride_axis=0)

    for d in range(NA_REL_BLOCKS):
        dblk = d - NA_REL_BLOCKS // 2
        halves = []
        for a in range(2):
            even = toeplitz(2 * dblk - a, 0)
            odd = toeplitz(2 * dblk + 1 - a, GRID_W)
            halves.append(jnp.where(col_ok, jnp.where(k_hi, odd, even), NEG_INF))
        o_ref[0, d] = jnp.concatenate(halves, axis=0) * LOG2E


def _na_table(rpb):
    heads, n_rows, n_cols = rpb.shape
    padded = jnp.pad(rpb, ((0, 0), (0, NA_RPB_ROWS_PADDED - n_rows), (0, Q_BLOCK - n_cols)))
    return pl.pallas_call(
        _na_table_kernel,
        grid=(heads,),
        in_specs=[pl.BlockSpec((1, NA_RPB_ROWS_PADDED, Q_BLOCK), lambda h: (h, 0, 0))],
        out_specs=pl.BlockSpec((1, NA_REL_BLOCKS, Q_BLOCK, Q_BLOCK), lambda h: (h, 0, 0, 0)),
        out_shape=jax.ShapeDtypeStruct((heads, NA_REL_BLOCKS, Q_BLOCK, Q_BLOCK), F32),
        compiler_params=_cparams(("arbitrary",)),
        name="na_table",
    )(padded)


def _rope(x, cos, sin_signed):
    lane = lax.broadcasted_iota(jnp.int32, x.shape, 1)
    even = jnp.bitwise_and(lane, 1) == 0
    swapped = jnp.where(even, pltpu.roll(x, HEAD_DIM - 1, 1), pltpu.roll(x, 1, 1))
    return x * cos + swapped * sin_signed


def _prep_kernel(ak_ref, bq_ref, bk_ref, ck_ref, dk_ref, cos_ref, sin_ref, qg_ref, kg_ref,
                 qo_ref, kt_ref):
    cos = cos_ref[...]
    sin = sin_ref[...]

    def head(ref, h):
        return ref[0, :, h * HEAD_DIM:(h + 1) * HEAD_DIM].astype(F32)

    for h in range(AX_HEADS):
        qh = _rope(_rms(head(bq_ref, h), qg_ref[...]), cos, sin)
        qo_ref[0, :, h * HEAD_DIM:(h + 1) * HEAD_DIM] = (qh * LOGIT_SCALE).astype(qo_ref.dtype)
    for h in range(NA_HEADS):
        kt_ref[0, KT_A + h] = head(ak_ref, h).T.astype(kt_ref.dtype)
    for h in range(AX_KV_HEADS):
        kh = _rope(_rms(head(bk_ref, h), kg_ref[...]), cos, sin)
        kt_ref[0, KT_B + h] = kh.T.astype(kt_ref.dtype)
    for h in range(SW_KV_HEADS):
        kt_ref[0, KT_C + h] = head(ck_ref, h).T.astype(kt_ref.dtype)
    for h in range(2 * DF_HEADS):
        kt_ref[0, KT_D + h] = head(dk_ref, h).T.astype(kt_ref.dtype)


def _prep(proj, cos, sin_signed, q_gain, k_gain, rows=512):
    b, s, _ = proj.shape
    kwb = AX_KV_HEADS * HEAD_DIM
    kwc = SW_KV_HEADS * HEAD_DIM
    return pl.pallas_call(
        _prep_kernel,
        grid=(b, s // rows),
        in_specs=[
            pl.BlockSpec((1, rows, GROUP_W), lambda bb, r: (bb, r, OFF_AK // GROUP_W)),
            pl.BlockSpec((1, rows, GROUP_W), lambda bb, r: (bb, r, OFF_BQ // GROUP_W)),
            pl.BlockSpec((1, rows, kwb), lambda bb, r: (bb, r, OFF_BK // kwb)),
            pl.BlockSpec((1, rows, kwc), lambda bb, r: (bb, r, OFF_CK // kwc)),
            pl.BlockSpec((1, rows, GROUP_W), lambda bb, r: (bb, r, OFF_DK // GROUP_W)),
            pl.BlockSpec((rows, HEAD_DIM), lambda bb, r: (r, 0)),
            pl.BlockSpec((rows, HEAD_DIM), lambda bb, r: (r, 0)),
            pl.BlockSpec((1, HEAD_DIM), lambda bb, r: (0, 0)),
            pl.BlockSpec((1, HEAD_DIM), lambda bb, r: (0, 0)),
        ],
        out_specs=[pl.BlockSpec((1, rows, GROUP_W), lambda bb, r: (bb, r, 0)),
                   pl.BlockSpec((1, KT_HEADS, HEAD_DIM, rows), lambda bb, r: (bb, 0, 0, r))],
        out_shape=[jax.ShapeDtypeStruct((b, s, GROUP_W), BF16),
                   jax.ShapeDtypeStruct((b, KT_HEADS, HEAD_DIM, s), BF16)],
        compiler_params=_cparams(("parallel", "parallel")),
        name="prep",
    )(proj, proj, proj, proj, proj, cos, sin_signed, q_gain.reshape(1, -1), k_gain.reshape(1, -1))


def _attn_a_kernel(q_ref, kt_ref, v_ref, tb_ref, o_ref, *, n_blocks, q_blocks):
    span = NA_SPAN_BLOCKS * Q_BLOCK
    grid_rows = 2 * n_blocks
    key_half = jnp.where(lax.broadcasted_iota(jnp.int32, (1, Q_BLOCK), 1) >= GRID_W, 1, 0)
    for c in range(q_blocks):
        i = pl.program_id(2) * q_blocks + c
        sb = jnp.clip(i - 2, 0, n_blocks - NA_SPAN_BLOCKS)
        start = pl.multiple_of(sb * Q_BLOCK, Q_BLOCK)
        rows = slice(c * Q_BLOCK, (c + 1) * Q_BLOCK)
        s = _dot(q_ref[0, rows, :], kt_ref[0, 0, :, pl.ds(start, span)])
        pieces = []
        for j in range(NA_SPAN_BLOCKS):
            blk = tb_ref[0, sb + j - i + NA_REL_BLOCKS // 2]
            key_row = 2 * (sb + j) + key_half
            halves = []
            for a in range(2):
                first = jnp.clip(2 * i + a - NA_WIN_ROWS // 2, 0, grid_rows - NA_WIN_ROWS)
                row_ok = (key_row >= first) & (key_row < first + NA_WIN_ROWS)
                halves.append(blk[a * GRID_W:(a + 1) * GRID_W] + jnp.where(row_ok, 0.0, NEG_INF))
            pieces.append(jnp.concatenate(halves, axis=0))
        e, l = _softmax_parts(s + jnp.concatenate(pieces, axis=1))
        o = _dot(e.astype(BF16), v_ref[0, pl.ds(start, span), :]) * (1.0 / l)
        o_ref[0, rows, :] = o.astype(o_ref.dtype)


def _attn_a(proj, kt, table, q_blocks=4):
    b, s, _ = proj.shape
    nb = s // Q_BLOCK
    hd = HEAD_DIM
    tq = q_blocks * Q_BLOCK
    return pl.pallas_call(
        functools.partial(_attn_a_kernel, n_blocks=nb, q_blocks=q_blocks),
        grid=(NA_HEADS, b, nb // q_blocks),
        in_specs=[
            pl.BlockSpec((1, tq, hd), lambda h, bb, i: (bb, i, OFF_AQ // hd + h)),
            pl.BlockSpec((1, 1, hd, s), lambda h, bb, i: (bb, KT_A + h, 0, 0)),
            pl.BlockSpec((1, s, hd), lambda h, bb, i: (bb, 0, OFF_AV // hd + h)),
            pl.BlockSpec((1, NA_REL_BLOCKS, Q_BLOCK, Q_BLOCK), lambda h, bb, i: (h, 0, 0, 0)),
        ],
        out_specs=pl.BlockSpec((1, tq, hd), lambda h, bb, i: (bb, i, MIX_A // hd + h)),
        out_shape=jax.ShapeDtypeStruct((b, s, D_MODEL), BF16),
        compiler_params=_cparams(("parallel", "parallel", "arbitrary")),
        name="attn_a",
    )(proj, kt, proj, table)


def _attn_b_kernel(q_ref, kt_ref, v_ref, mix_ref, o_ref):
    del mix_ref
    kt = kt_ref[0, 0]
    v = v_ref[0]
    for g in range(AX_HEADS // AX_KV_HEADS):
        cols = slice(g * HEAD_DIM, (g + 1) * HEAD_DIM)
        e, l = _softmax_parts(_dot(q_ref[0, :, cols], kt))
        o = _dot(e.astype(BF16), v) * (1.0 / l)
        o_ref[0, :, cols] = o.astype(o_ref.dtype)


def _attn_b(qn, kt, proj, mix, tq=256):
    b, s, _ = proj.shape
    qw = (AX_HEADS // AX_KV_HEADS) * HEAD_DIM
    return pl.pallas_call(
        _attn_b_kernel,
        grid=(b, AX_KV_HEADS, s // tq),
        in_specs=[
            pl.BlockSpec((1, tq, qw), lambda bb, kv, i: (bb, i, kv)),
            pl.BlockSpec((1, 1, HEAD_DIM, s), lambda bb, kv, i: (bb, KT_B + kv, 0, 0)),
            pl.BlockSpec((1, s, HEAD_DIM), lambda bb, kv, i: (bb, 0, OFF_BV // HEAD_DIM + kv)),
            pl.BlockSpec(memory_space=pl.ANY),
        ],
        out_specs=pl.BlockSpec((1, tq, qw), lambda bb, kv, i: (bb, i, MIX_B // qw + kv)),
        out_shape=jax.ShapeDtypeStruct(mix.shape, mix.dtype),
        input_output_aliases={3: 0},
        compiler_params=_cparams(("parallel", "parallel", "arbitrary")),
        name="attn_b",
    )(qn, kt, proj, mix)


def _attn_c_kernel(q_ref, ktp_ref, ktc_ref, ktn_ref, vp_ref, vc_ref, vn_ref, t5_ref, sink_ref,
                   mix_ref, o_ref, *, n_blocks):
    del mix_ref
    kv = pl.program_id(1)
    i = pl.program_id(2)
    groups = SW_HEADS // SW_KV_HEADS
    kinds = (jnp.where(i > 0, T5_BAND0, T5_MASKED), T5_BAND0 + 1,
             jnp.where(i < n_blocks - 1, T5_BAND0 + 2, T5_MASKED))
    q = jnp.concatenate(
        [q_ref[0, :, g * HEAD_DIM:(g + 1) * HEAD_DIM] for g in range(groups)], axis=0)
    s = jnp.concatenate(
        [_dot(q, kt_ref[0, 0]) + jnp.concatenate([t5_ref[g, kind] for g in range(groups)], axis=0)
         for kt_ref, kind in zip((ktp_ref, ktc_ref, ktn_ref), kinds)], axis=1)
    sink = jnp.concatenate(
        [jnp.full((Q_BLOCK, 1), sink_ref[kv * groups + g] * LOG2E, F32) for g in range(groups)],
        axis=0)
    e, l = _softmax_parts(s, extra=sink)
    v = jnp.concatenate([vp_ref[0], vc_ref[0], vn_ref[0]], axis=0)
    o = _dot(e.astype(BF16), v) * (1.0 / l)
    for g in range(groups):
        o_ref[0, :, g * HEAD_DIM:(g + 1) * HEAD_DIM] = (
            o[g * Q_BLOCK:(g + 1) * Q_BLOCK].astype(o_ref.dtype))


def _attn_c(proj, kt, t5_blocks, sink, mix):
    b, s, _ = proj.shape
    nb = s // Q_BLOCK
    groups = SW_HEADS // SW_KV_HEADS
    qw = groups * HEAD_DIM
    hd = HEAD_DIM

    def prev(i):
        return jnp.maximum(i - 1, 0)

    def nxt(i):
        return jnp.minimum(i + 1, nb - 1)

    kt_specs = [pl.BlockSpec((1, 1, hd, Q_BLOCK), lambda bb, kv, i, f=f: (bb, KT_C + kv, 0, f(i)))
                for f in (prev, lambda i: i, nxt)]
    v_specs = [pl.BlockSpec((1, Q_BLOCK, hd), lambda bb, kv, i, f=f: (bb, f(i), OFF_CV // hd + kv))
               for f in (prev, lambda i: i, nxt)]
    return pl.pallas_call(
        functools.partial(_attn_c_kernel, n_blocks=nb),
        grid=(b, SW_KV_HEADS, nb),
        in_specs=[pl.BlockSpec((1, Q_BLOCK, qw), lambda bb, kv, i: (bb, i, OFF_CQ // qw + kv))]
        + kt_specs + v_specs
        + [pl.BlockSpec((groups, T5_KINDS, Q_BLOCK, Q_BLOCK), lambda bb, kv, i: (kv, 0, 0, 0)),
           pl.BlockSpec(memory_space=pltpu.SMEM),
           pl.BlockSpec(memory_space=pl.ANY)],
        out_specs=pl.BlockSpec((1, Q_BLOCK, qw), lambda bb, kv, i: (bb, i, MIX_C // qw + kv)),
        out_shape=jax.ShapeDtypeStruct(mix.shape, mix.dtype),
        input_output_aliases={9: 0},
        compiler_params=_cparams(("parallel", "parallel", "arbitrary")),
        name="attn_c",
    )(proj, kt, kt, kt, proj, proj, proj, t5_blocks, sink, mix)


def _attn_d_kernel(q1_ref, q2_ref, k1t_ref, k2t_ref, v_ref, t5_ref, lam_ref, g_ref, mix_ref, o_ref,
                   *, n_blocks, q_blocks, lambda_init):
    del mix_ref
    lp = lam_ref[...]
    lam = (jnp.exp(jnp.sum(lp[0:1] * lp[1:2], axis=-1, keepdims=True))
           - jnp.exp(jnp.sum(lp[2:3] * lp[3:4], axis=-1, keepdims=True)) + lambda_init)
    k1t = k1t_ref[0, 0]
    k2t = k2t_ref[0, 0]
    v = v_ref[0]
    for c in range(q_blocks):
        i = pl.program_id(2) * q_blocks + c
        rows = slice(c * Q_BLOCK, (c + 1) * Q_BLOCK)
        bias = jnp.concatenate(
            [t5_ref[0, jnp.clip(kj - i + T5_BAND0 + 1, T5_FAR_LO, T5_FAR_HI)]
             for kj in range(n_blocks)], axis=1)
        e1, l1 = _softmax_parts(_dot(q1_ref[0, rows, :], k1t) + bias)
        e2, l2 = _softmax_parts(_dot(q2_ref[0, rows, :], k2t) + bias)
        ev = _dot(jnp.concatenate([e1.astype(BF16), e2.astype(BF16)], axis=0), v)
        od = ev[:Q_BLOCK] * (1.0 / l1) - ev[Q_BLOCK:] * (lam / l2)
        o_ref[0, rows, :] = (_rms(od, g_ref[...]) * (1.0 - lambda_init)).astype(o_ref.dtype)


def _attn_d(proj, kt, t5_blocks, lam_params, subln, lambda_init, mix, q_blocks=8):
    b, s, _ = proj.shape
    nb = s // Q_BLOCK
    hd = HEAD_DIM
    tq = q_blocks * Q_BLOCK
    return pl.pallas_call(
        functools.partial(_attn_d_kernel, n_blocks=nb, q_blocks=q_blocks, lambda_init=lambda_init),
        grid=(b, DF_HEADS, nb // q_blocks),
        in_specs=[
            pl.BlockSpec((1, tq, hd), lambda bb, h, i: (bb, i, OFF_DQ // hd + h)),
            pl.BlockSpec((1, tq, hd), lambda bb, h, i: (bb, i, OFF_DQ // hd + DF_HEADS + h)),
            pl.BlockSpec((1, 1, hd, s), lambda bb, h, i: (bb, KT_D + h, 0, 0)),
            pl.BlockSpec((1, 1, hd, s), lambda bb, h, i: (bb, KT_D + DF_HEADS + h, 0, 0)),
            pl.BlockSpec((1, s, DF_V_DIM), lambda bb, h, i: (bb, 0, OFF_DV // DF_V_DIM + h)),
            pl.BlockSpec((1, T5_KINDS, Q_BLOCK, Q_BLOCK), lambda bb, h, i: (SW_HEADS + h, 0, 0, 0)),
            pl.BlockSpec((4, hd), lambda bb, h, i: (0, 0)),
            pl.BlockSpec((1, DF_V_DIM), lambda bb, h, i: (0, 0)),
            pl.BlockSpec(memory_space=pl.ANY),
        ],
        out_specs=pl.BlockSpec((1, tq, DF_V_DIM), lambda bb, h, i: (bb, i, MIX_D // DF_V_DIM + h)),
        out_shape=jax.ShapeDtypeStruct(mix.shape, mix.dtype),
        input_output_aliases={8: 0},
        compiler_params=_cparams(("parallel", "parallel", "arbitrary")),
        name="attn_d",
    )(proj, proj, kt, kt, proj, t5_blocks, lam_params, subln.reshape(1, -1), mix)


def _rope_tables(s):
    pos = jnp.arange(s, dtype=jnp.int32)
    row = (pos // GRID_W).astype(F32)
    col = (pos % GRID_W).astype(F32)
    n_pairs = HEAD_DIM // 4
    inv = ROPE_THETA ** (-jnp.arange(n_pairs, dtype=F32) / n_pairs)
    ang = jnp.concatenate([row[:, None] * inv, col[:, None] * inv], axis=-1)
    cos = jnp.repeat(jnp.cos(ang), 2, axis=-1)
    sin = jnp.repeat(jnp.sin(ang), 2, axis=-1)
    sign = jnp.tile(jnp.asarray([-1.0, 1.0], F32), HEAD_DIM // 2)
    return cos, sin * sign


def _q_col_scale():
    col = jnp.arange(D_IN, dtype=jnp.int32)
    is_q = (((col >= OFF_AQ) & (col < OFF_AK)) | ((col >= OFF_CQ) & (col < OFF_CK))
            | ((col >= OFF_DQ) & (col < OFF_DK)))
    return jnp.where(is_q, LOGIT_SCALE, 1.0).astype(F32).reshape(1, D_IN)


def kernel(x, ln_attn_pre, ln_attn_post, ln_mlp_pre, ln_mlp_post, w_in, w_out, na_rpb,
           ax_q_norm, ax_k_norm, sw_sink, df_lambda, df_subln, t5_table, w_mlp_in, w_mlp_out):
    b, s, d = x.shape
    depth = w_in.shape[0]
    t = b * s
    cos, sin_signed = _rope_tables(s)
    col_scale = _q_col_scale()
    t5_blocks = _t5_blocks(t5_table)
    xf = x.reshape(t, d)
    h = _norm_cast(xf, ln_attn_pre[0])
    for l in range(depth):
        proj = _matmul(h, w_in, l, BF16, col_scale=col_scale, name="in_proj").reshape(b, s, D_IN)
        qn_b, kt = _prep(proj, cos, sin_signed, ax_q_norm[l], ax_k_norm[l])
        mix = _attn_a(proj, kt, _na_table(na_rpb[l]))
        mix = _attn_b(qn_b, kt, proj, mix)
        mix = _attn_c(proj, kt, t5_blocks, sw_sink[l], mix)
        lambda_init = 0.8 - 0.6 * math.exp(-0.3 * l)
        mix = _attn_d(proj, kt, t5_blocks, df_lambda[l], df_subln[l], lambda_init, mix)
        y = _matmul(mix.reshape(t, d), w_out, l, BF16, name="out_proj")
        xf, h = _resid_norm(xf, y, ln_attn_post[l], ln_mlp_pre[l])
        u = _matmul(h, w_mlp_in, l, BF16, act=_relu2, name="mlp_in")
        y = _matmul_ksplit(u, w_mlp_out, l, BF16, name="mlp_out")
        g_next = ln_attn_pre[l + 1] if l + 1 < depth else None
        xf, h = _resid_norm(xf, y, ln_mlp_post[l], g_next)
    return xf.reshape(b, s, d)
```

```python
import functools
import math

import jax
import jax.numpy as jnp
from jax import lax
from jax.experimental import pallas as pl
from jax.experimental.pallas import tpu as pltpu

F32 = jnp.float32
BF16 = jnp.bfloat16

D_MODEL = 4096
GRID_W = 64
HEAD_DIM = 128
Q_BLOCK = 128
GROUP_W = D_MODEL // 4
NA_HEADS = GROUP_W // HEAD_DIM
NA_WIN_ROWS = 8
NA_WIN_COLS = 16
AX_HEADS = GROUP_W // HEAD_DIM
AX_KV_HEADS = AX_HEADS // 4
ROPE_THETA = 10000.0
SW_HEADS = GROUP_W // HEAD_DIM
SW_KV_HEADS = SW_HEADS // 4
SW_WINDOW = 128
DF_V_DIM = 2 * HEAD_DIM
DF_HEADS = GROUP_W // DF_V_DIM
T5_BUCKETS = 32
T5_MAX_DIST = 128
T5_HEADS = SW_HEADS + DF_HEADS
EPS = 1e-6
SCALE = HEAD_DIM ** -0.5
LOG2E = math.log2(math.e)
LOGIT_SCALE = SCALE * LOG2E

OFF_AQ, OFF_AK, OFF_AV = 0, 1024, 2048
OFF_BQ, OFF_BK, OFF_BV = 3072, 4096, 4352
OFF_CQ, OFF_CK, OFF_CV = 4608, 5632, 5888
OFF_DQ, OFF_DK, OFF_DV = 6144, 7168, 8192
D_IN = 9216
MIX_A, MIX_B, MIX_C, MIX_D = 0, GROUP_W, 2 * GROUP_W, 3 * GROUP_W

KT_A, KT_B, KT_C, KT_D = 0, NA_HEADS, NA_HEADS + AX_KV_HEADS, NA_HEADS + AX_KV_HEADS + SW_KV_HEADS
KT_HEADS = KT_D + 2 * DF_HEADS

NA_SPAN_BLOCKS = 5
NA_REL_BLOCKS = 9
NA_RPB_ROWS = 2 * NA_WIN_ROWS - 1
NA_RPB_COLS = 2 * NA_WIN_COLS - 1
NA_RPB_ROWS_PADDED = 16

T5_FAR_LO, T5_BAND0, T5_FAR_HI, T5_MASKED, T5_KINDS = 0, 1, 4, 5, 6

VMEM_LIMIT = 60 * 1024 * 1024
NEG_INF = float("-inf")


def _cparams(sem):
    return pltpu.CompilerParams(dimension_semantics=sem, vmem_limit_bytes=VMEM_LIMIT)


def _dot(a, b):
    return jnp.dot(a, b, preferred_element_type=F32)


def _rms(x, g):
    return x * lax.rsqrt(jnp.mean(x * x, axis=-1, keepdims=True) + EPS) * g


def _softmax_parts(s, extra=None):
    m = jnp.max(s, axis=-1, keepdims=True)
    if extra is not None:
        m = jnp.maximum(m, extra)
    e = jnp.exp2(s - m)
    l = jnp.sum(e, axis=-1, keepdims=True)
    if extra is not None:
        l = l + jnp.exp2(extra - m)
    return e, l


def _norm_cast_kernel(x_ref, g_ref, h_ref):
    h_ref[...] = _rms(x_ref[...], g_ref[...]).astype(h_ref.dtype)


def _norm_cast(x, g, rows=256):
    t, d = x.shape
    return pl.pallas_call(
        _norm_cast_kernel,
        grid=(t // rows,),
        in_specs=[pl.BlockSpec((rows, d), lambda i: (i, 0)),
                  pl.BlockSpec((1, d), lambda i: (0, 0))],
        out_specs=pl.BlockSpec((rows, d), lambda i: (i, 0)),
        out_shape=jax.ShapeDtypeStruct((t, d), BF16),
        compiler_params=_cparams(("parallel",)),
        name="norm_cast",
    )(x, g.reshape(1, d))


def _resid_norm_kernel(x_ref, y_ref, gp_ref, gn_ref, xo_ref, h_ref):
    xn = x_ref[...] + _rms(y_ref[...].astype(F32), gp_ref[...])
    xo_ref[...] = xn
    h_ref[...] = _rms(xn, gn_ref[...]).astype(h_ref.dtype)


def _resid_kernel(x_ref, y_ref, gp_ref, xo_ref):
    xo_ref[...] = x_ref[...] + _rms(y_ref[...].astype(F32), gp_ref[...])


def _resid_norm(x, y, g_post, g_next, rows=256):
    t, d = x.shape
    row_spec = pl.BlockSpec((rows, d), lambda i: (i, 0))
    g_spec = pl.BlockSpec((1, d), lambda i: (0, 0))
    if g_next is None:
        return pl.pallas_call(
            _resid_kernel, grid=(t // rows,),
            in_specs=[row_spec, row_spec, g_spec], out_specs=row_spec,
            out_shape=jax.ShapeDtypeStruct((t, d), F32),
            compiler_params=_cparams(("parallel",)), name="resid",
        )(x, y, g_post.reshape(1, d)), None
    return pl.pallas_call(
        _resid_norm_kernel, grid=(t // rows,),
        in_specs=[row_spec, row_spec, g_spec, g_spec], out_specs=[row_spec, row_spec],
        out_shape=[jax.ShapeDtypeStruct((t, d), F32), jax.ShapeDtypeStruct((t, d), BF16)],
        compiler_params=_cparams(("parallel",)), name="resid_norm",
    )(x, y, g_post.reshape(1, d), g_next.reshape(1, d))


def _relu2(r):
    r = jnp.maximum(r, 0.0)
    return r * r


def _mm_kernel(a_ref, w_ref, *rest, act, col_scaled):
    o_ref = rest[-1]
    r = _dot(a_ref[...], w_ref[...].astype(BF16))
    if act is not None:
        r = act(r)
    if col_scaled:
        r = r * rest[0][...]
    o_ref[...] = r.astype(o_ref.dtype)


def _matmul(a, w, layer, out_dtype, *, tm=2048, tn=512, act=None, col_scale=None, name="matmul"):
    m, kdim = a.shape
    _, _, n = w.shape
    in_specs = [pl.BlockSpec((tm, kdim), lambda i, j: (i, 0)),
                pl.BlockSpec((None, kdim, tn), lambda i, j: (layer, 0, j))]
    args = [a, w]
    if col_scale is not None:
        in_specs.append(pl.BlockSpec((1, tn), lambda i, j: (0, j)))
        args.append(col_scale)
    return pl.pallas_call(
        functools.partial(_mm_kernel, act=act, col_scaled=col_scale is not None),
        grid=(m // tm, n // tn),
        in_specs=in_specs,
        out_specs=pl.BlockSpec((tm, tn), lambda i, j: (i, j)),
        out_shape=jax.ShapeDtypeStruct((m, n), out_dtype),
        compiler_params=_cparams(("parallel", "parallel")),
        name=name,
    )(*args)


def _mm_ksplit_kernel(a_ref, w_ref, o_ref, acc_ref):
    k = pl.program_id(2)

    @pl.when(k == 0)
    def _():
        acc_ref[...] = jnp.zeros(acc_ref.shape, acc_ref.dtype)

    acc_ref[...] += _dot(a_ref[...], w_ref[...].astype(BF16))

    @pl.when(k == pl.num_programs(2) - 1)
    def _():
        o_ref[...] = acc_ref[...].astype(o_ref.dtype)


def _matmul_ksplit(a, w, layer, out_dtype, *, tm=2048, tn=1024, tk=2048, name="matmul_ksplit"):
    m, kdim = a.shape
    _, _, n = w.shape
    return pl.pallas_call(
        _mm_ksplit_kernel,
        grid=(m // tm, n // tn, kdim // tk),
        in_specs=[pl.BlockSpec((tm, tk), lambda i, j, k: (i, k)),
                  pl.BlockSpec((None, tk, tn), lambda i, j, k: (layer, k, j))],
        out_specs=pl.BlockSpec((tm, tn), lambda i, j, k: (i, j)),
        out_shape=jax.ShapeDtypeStruct((m, n), out_dtype),
        scratch_shapes=[pltpu.VMEM((tm, tn), F32)],
        compiler_params=_cparams(("parallel", "parallel", "arbitrary")),
        name=name,
    )(a, w)


def _t5_bucket(rel):
    nb = T5_BUCKETS // 2
    max_exact = nb // 2
    base = jnp.where(rel > 0, nb, 0)
    n = jnp.abs(rel)
    n_f = jnp.maximum(n, 1).astype(F32)
    large = max_exact + (jnp.log(n_f / max_exact) / math.log(T5_MAX_DIST / max_exact)
                         * (nb - max_exact)).astype(jnp.int32)
    large = jnp.minimum(large, nb - 1)
    return base + jnp.where(n < max_exact, n, large)


def _t5_blocks_kernel(idx_ref, tab_ref, o_ref):
    h = pl.program_id(0)
    shape = (Q_BLOCK, Q_BLOCK)
    t = lax.broadcasted_iota(jnp.int32, shape, 0)
    u = lax.broadcasted_iota(jnp.int32, shape, 1)
    o_ref[0, T5_FAR_LO] = jnp.full(shape, tab_ref[T5_BUCKETS // 2 - 1, h] * LOG2E, F32)
    o_ref[0, T5_FAR_HI] = jnp.full(shape, tab_ref[T5_BUCKETS - 1, h] * LOG2E, F32)
    o_ref[0, T5_MASKED] = jnp.full(shape, NEG_INF, F32)
    for j in range(3):
        idx = idx_ref[j]
        diag = jnp.zeros(idx.shape, F32)
        for b in range(T5_BUCKETS):
            diag = jnp.where(idx == b, tab_ref[b, h], diag)
        upper = pltpu.roll(jnp.broadcast_to(diag[0:1], shape), 0, 1, stride=1, stride_axis=0)
        lower = pltpu.roll(jnp.broadcast_to(diag[1:2], shape), 0, 1, stride=1, stride_axis=0)
        band = jnp.where(u >= t, upper, lower)
        rel = u + (j - 1) * Q_BLOCK - t
        allowed = (jnp.abs(rel) <= SW_WINDOW) | (h >= SW_HEADS)
        o_ref[0, T5_BAND0 + j] = jnp.where(allowed, band, NEG_INF) * LOG2E


def _t5_blocks(t5_table):
    k = jnp.arange(Q_BLOCK, dtype=jnp.int32)
    block_off = (jnp.arange(3, dtype=jnp.int32) - 1) * Q_BLOCK
    rel = block_off[:, None, None] + jnp.stack([k, k - Q_BLOCK])[None]
    idx = _t5_bucket(rel).astype(jnp.int32)
    return pl.pallas_call(
        _t5_blocks_kernel,
        grid=(T5_HEADS,),
        in_specs=[pl.BlockSpec((3, 2, Q_BLOCK), lambda h: (0, 0, 0)),
                  pl.BlockSpec(memory_space=pltpu.SMEM)],
        out_specs=pl.BlockSpec((1, T5_KINDS, Q_BLOCK, Q_BLOCK), lambda h: (h, 0, 0, 0)),
        out_shape=jax.ShapeDtypeStruct((T5_HEADS, T5_KINDS, Q_BLOCK, Q_BLOCK), F32),
        compiler_params=_cparams(("arbitrary",)),
        name="t5_blocks",
    )(idx, t5_table)


def _na_table_kernel(rpb_ref, o_ref):
    shape = (GRID_W, Q_BLOCK)
    c = lax.broadcasted_iota(jnp.int32, shape, 0)
    u = lax.broadcasted_iota(jnp.int32, shape, 1)
    k_hi = u >= GRID_W
    kc = jnp.bitwise_and(u, GRID_W - 1)
    cs = jnp.clip(c - NA_WIN_COLS // 2, 0, GRID_W - NA_WIN_COLS)
    col_ok = (kc >= cs) & (kc < cs + NA_WIN_COLS)

    def toeplitz(dr, lane_off):
        if abs(dr) > NA_WIN_ROWS - 1:
            return jnp.full(shape, NEG_INF, F32)
        row = dr + NA_WIN_ROWS - 1
        vec = jnp.broadcast_to(rpb_ref[0, row:row + 1, :], shape)
        shift = (lane_off - (NA_WIN_COLS - 1)) % Q_BLOCK
        return pltpu.roll(vec, shift, 1, stride=1, stride_axis=0)

    for d in range(NA_REL_BLOCKS):
        dblk = d - NA_REL_BLOCKS // 2
        halves = []
        for a in range(2):
            even = toeplitz(2 * dblk - a, 0)
            odd = toeplitz(2 * dblk + 1 - a, GRID_W)
            halves.append(jnp.where(col_ok, jnp.where(k_hi, odd, even), NEG_INF))
        o_ref[0, d] = jnp.concatenate(halves, axis=0) * LOG2E


def _na_table(rpb):
    heads, n_rows, n_cols = rpb.shape
    padded = jnp.pad(rpb, ((0, 0), (0, NA_RPB_ROWS_PADDED - n_rows), (0, Q_BLOCK - n_cols)))
    return pl.pallas_call(
        _na_table_kernel,
        grid=(heads,),
        in_specs=[pl.BlockSpec((1, NA_RPB_ROWS_PADDED, Q_BLOCK), lambda h: (h, 0, 0))],
        out_specs=pl.BlockSpec((1, NA_REL_BLOCKS, Q_BLOCK, Q_BLOCK), lambda h: (h, 0, 0, 0)),
        out_shape=jax.ShapeDtypeStruct((heads, NA_REL_BLOCKS, Q_BLOCK, Q_BLOCK), F32),
        compiler_params=_cparams(("arbitrary",)),
        name="na_table",
    )(padded)


def _rope(x, cos, sin_signed):
    lane = lax.broadcasted_iota(jnp.int32, x.shape, 1)
    even = jnp.bitwise_and(lane, 1) == 0
    swapped = jnp.where(even, pltpu.roll(x, HEAD_DIM - 1, 1), pltpu.roll(x, 1, 1))
    return x * cos + swapped * sin_signed


def _prep_kernel(ak_ref, bq_ref, bk_ref, ck_ref, dk_ref, cos_ref, sin_ref, qg_ref, kg_ref,
                 qo_ref, kt_ref):
    cos = cos_ref[...]
    sin = sin_ref[...]

    def head(ref, h):
        return ref[0, :, h * HEAD_DIM:(h + 1) * HEAD_DIM].astype(F32)

    for h in range(AX_HEADS):
        qh = _rope(_rms(head(bq_ref, h), qg_ref[...]), cos, sin)
        qo_ref[0, :, h * HEAD_DIM:(h + 1) * HEAD_DIM] = (qh * LOGIT_SCALE).astype(qo_ref.dtype)
    for h in range(NA_HEADS):
        kt_ref[0, KT_A + h] = head(ak_ref, h).T.astype(kt_ref.dtype)
    for h in range(AX_KV_HEADS):
        kh = _rope(_rms(head(bk_ref, h), kg_ref[...]), cos, sin)
        kt_ref[0, KT_B + h] = kh.T.astype(kt_ref.dtype)
    for h in range(SW_KV_HEADS):
        kt_ref[0, KT_C + h] = head(ck_ref, h).T.astype(kt_ref.dtype)
    for h in range(2 * DF_HEADS):
        kt_ref[0, KT_D + h] = head(dk_ref, h).T.astype(kt_ref.dtype)


def _prep(proj, cos, sin_signed, q_gain, k_gain, rows=512):
    b, s, _ = proj.shape
    kwb = AX_KV_HEADS * HEAD_DIM
    kwc = SW_KV_HEADS * HEAD_DIM
    return pl.pallas_call(
        _prep_kernel,
        grid=(b, s // rows),
        in_specs=[
            pl.BlockSpec((1, rows, GROUP_W), lambda bb, r: (bb, r, OFF_AK // GROUP_W)),
            pl.BlockSpec((1, rows, GROUP_W), lambda bb, r: (bb, r, OFF_BQ // GROUP_W)),
            pl.BlockSpec((1, rows, kwb), lambda bb, r: (bb, r, OFF_BK // kwb)),
            pl.BlockSpec((1, rows, kwc), lambda bb, r: (bb, r, OFF_CK // kwc)),
            pl.BlockSpec((1, rows, GROUP_W), lambda bb, r: (bb, r, OFF_DK // GROUP_W)),
            pl.BlockSpec((rows, HEAD_DIM), lambda bb, r: (r, 0)),
            pl.BlockSpec((rows, HEAD_DIM), lambda bb, r: (r, 0)),
            pl.BlockSpec((1, HEAD_DIM), lambda bb, r: (0, 0)),
            pl.BlockSpec((1, HEAD_DIM), lambda bb, r: (0, 0)),
        ],
        out_specs=[pl.BlockSpec((1, rows, GROUP_W), lambda bb, r: (bb, r, 0)),
                   pl.BlockSpec((1, KT_HEADS, HEAD_DIM, rows), lambda bb, r: (bb, 0, 0, r))],
        out_shape=[jax.ShapeDtypeStruct((b, s, GROUP_W), BF16),
                   jax.ShapeDtypeStruct((b, KT_HEADS, HEAD_DIM, s), BF16)],
        compiler_params=_cparams(("parallel", "parallel")),
        name="prep",
    )(proj, proj, proj, proj, proj, cos, sin_signed, q_gain.reshape(1, -1), k_gain.reshape(1, -1))


def _attn_a_kernel(q_ref, kt_ref, v_ref, tb_ref, o_ref, *, n_blocks, q_blocks):
    span = NA_SPAN_BLOCKS * Q_BLOCK
    grid_rows = 2 * n_blocks
    key_half = jnp.where(lax.broadcasted_iota(jnp.int32, (1, Q_BLOCK), 1) >= GRID_W, 1, 0)
    for c in range(q_blocks):
        i = pl.program_id(2) * q_blocks + c
        sb = jnp.clip(i - 2, 0, n_blocks - NA_SPAN_BLOCKS)
        start = pl.multiple_of(sb * Q_BLOCK, Q_BLOCK)
        rows = slice(c * Q_BLOCK, (c + 1) * Q_BLOCK)
        s = _dot(q_ref[0, rows, :], kt_ref[0, 0, :, pl.ds(start, span)])
        pieces = []
        for j in range(NA_SPAN_BLOCKS):
            blk = tb_ref[0, sb + j - i + NA_REL_BLOCKS // 2]
            key_row = 2 * (sb + j) + key_half
            halves = []
            for a in range(2):
                first = jnp.clip(2 * i + a - NA_WIN_ROWS // 2, 0, grid_rows - NA_WIN_ROWS)
                row_ok = (key_row >= first) & (key_row < first + NA_WIN_ROWS)
                halves.append(blk[a * GRID_W:(a + 1) * GRID_W] + jnp.where(row_ok, 0.0, NEG_INF))
            pieces.append(jnp.concatenate(halves, axis=0))
        e, l = _softmax_parts(s + jnp.concatenate(pieces, axis=1))
        o = _dot(e.astype(BF16), v_ref[0, pl.ds(start, span), :]) * (1.0 / l)
        o_ref[0, rows, :] = o.astype(o_ref.dtype)


def _attn_a(proj, kt, table, q_blocks=16):
    b, s, _ = proj.shape
    nb = s // Q_BLOCK
    hd = HEAD_DIM
    tq = q_blocks * Q_BLOCK
    return pl.pallas_call(
        functools.partial(_attn_a_kernel, n_blocks=nb, q_blocks=q_blocks),
        grid=(NA_HEADS, b, nb // q_blocks),
        in_specs=[
            pl.BlockSpec((1, tq, hd), lambda h, bb, i: (bb, i, OFF_AQ // hd + h)),
            pl.BlockSpec((1, 1, hd, s), lambda h, bb, i: (bb, KT_A + h, 0, 0)),
            pl.BlockSpec((1, s, hd), lambda h, bb, i: (bb, 0, OFF_AV // hd + h)),
            pl.BlockSpec((1, NA_REL_BLOCKS, Q_BLOCK, Q_BLOCK), lambda h, bb, i: (h, 0, 0, 0)),
        ],
        out_specs=pl.BlockSpec((1, tq, hd), lambda h, bb, i: (bb, i, MIX_A // hd + h)),
        out_shape=jax.ShapeDtypeStruct((b, s, D_MODEL), BF16),
        compiler_params=_cparams(("parallel", "parallel", "arbitrary")),
        name="attn_a",
    )(proj, kt, proj, table)


def _attn_b_kernel(q_ref, kt_ref, v_ref, mix_ref, o_ref):
    del mix_ref
    kt = kt_ref[0, 0]
    v = v_ref[0]
    for g in range(AX_HEADS // AX_KV_HEADS):
        cols = slice(g * HEAD_DIM, (g + 1) * HEAD_DIM)
        e, l = _softmax_parts(_dot(q_ref[0, :, cols], kt))
        o = _dot(e.astype(BF16), v) * (1.0 / l)
        o_ref[0, :, cols] = o.astype(o_ref.dtype)


def _attn_b(qn, kt, proj, mix, tq=512):
    b, s, _ = proj.shape
    qw = (AX_HEADS // AX_KV_HEADS) * HEAD_DIM
    return pl.pallas_call(
        _attn_b_kernel,
        grid=(b, AX_KV_HEADS, s // tq),
        in_specs=[
            pl.BlockSpec((1, tq, qw), lambda bb, kv, i: (bb, i, kv)),
            pl.BlockSpec((1, 1, HEAD_DIM, s), lambda bb, kv, i: (bb, KT_B + kv, 0, 0)),
            pl.BlockSpec((1, s, HEAD_DIM), lambda bb, kv, i: (bb, 0, OFF_BV // HEAD_DIM + kv)),
            pl.BlockSpec(memory_space=pl.ANY),
        ],
        out_specs=pl.BlockSpec((1, tq, qw), lambda bb, kv, i: (bb, i, MIX_B // qw + kv)),
        out_shape=jax.ShapeDtypeStruct(mix.shape, mix.dtype),
        input_output_aliases={3: 0},
        compiler_params=_cparams(("parallel", "parallel", "arbitrary")),
        name="attn_b",
    )(qn, kt, proj, mix)


def _attn_c_kernel(q_ref, kt_ref, v_ref, t5_ref, sink_ref, mix_ref, o_ref, *, n_blocks, q_blocks):
    del mix_ref
    kv = pl.program_id(1)
    groups = SW_HEADS // SW_KV_HEADS
    sink = jnp.concatenate(
        [jnp.full((Q_BLOCK, 1), sink_ref[kv * groups + g] * LOG2E, F32) for g in range(groups)],
        axis=0)
    for c in range(q_blocks):
        i = pl.program_id(2) * q_blocks + c
        rows = slice(c * Q_BLOCK, (c + 1) * Q_BLOCK)
        blocks = (jnp.maximum(i - 1, 0), i, jnp.minimum(i + 1, n_blocks - 1))
        kinds = (jnp.where(i > 0, T5_BAND0, T5_MASKED), T5_BAND0 + 1,
                 jnp.where(i < n_blocks - 1, T5_BAND0 + 2, T5_MASKED))
        starts = [pl.multiple_of(blk * Q_BLOCK, Q_BLOCK) for blk in blocks]
        q = jnp.concatenate(
            [q_ref[0, rows, g * HEAD_DIM:(g + 1) * HEAD_DIM] for g in range(groups)], axis=0)
        s = jnp.concatenate(
            [_dot(q, kt_ref[0, 0, :, pl.ds(start, Q_BLOCK)])
             + jnp.concatenate([t5_ref[g, kind] for g in range(groups)], axis=0)
             for start, kind in zip(starts, kinds)], axis=1)
        e, l = _softmax_parts(s, extra=sink)
        v = jnp.concatenate([v_ref[0, pl.ds(start, Q_BLOCK), :] for start in starts], axis=0)
        o = _dot(e.astype(BF16), v) * (1.0 / l)
        for g in range(groups):
            o_ref[0, rows, g * HEAD_DIM:(g + 1) * HEAD_DIM] = (
                o[g * Q_BLOCK:(g + 1) * Q_BLOCK].astype(o_ref.dtype))


def _attn_c(proj, kt, t5_blocks, sink, mix, q_blocks=8):
    b, s, _ = proj.shape
    nb = s // Q_BLOCK
    groups = SW_HEADS // SW_KV_HEADS
    qw = groups * HEAD_DIM
    hd = HEAD_DIM
    tq = q_blocks * Q_BLOCK
    return pl.pallas_call(
        functools.partial(_attn_c_kernel, n_blocks=nb, q_blocks=q_blocks),
        grid=(b, SW_KV_HEADS, nb // q_blocks),
        in_specs=[
            pl.BlockSpec((1, tq, qw), lambda bb, kv, i: (bb, i, OFF_CQ // qw + kv)),
            pl.BlockSpec((1, 1, hd, s), lambda bb, kv, i: (bb, KT_C + kv, 0, 0)),
            pl.BlockSpec((1, s, hd), lambda bb, kv, i: (bb, 0, OFF_CV // hd + kv)),
            pl.BlockSpec((groups, T5_KINDS, Q_BLOCK, Q_BLOCK), lambda bb, kv, i: (kv, 0, 0, 0)),
            pl.BlockSpec(memory_space=pltpu.SMEM),
            pl.BlockSpec(memory_space=pl.ANY),
        ],
        out_specs=pl.BlockSpec((1, tq, qw), lambda bb, kv, i: (bb, i, MIX_C // qw + kv)),
        out_shape=jax.ShapeDtypeStruct(mix.shape, mix.dtype),
        input_output_aliases={5: 0},
        compiler_params=_cparams(("parallel", "parallel", "arbitrary")),
        name="attn_c",
    )(proj, kt, proj, t5_blocks, sink, mix)


def _attn_d_kernel(q1_ref, q2_ref, k1t_ref, k2t_ref, v_ref, t5_ref, lam_ref, g_ref, mix_ref, o_ref,
                   *, n_blocks, q_blocks, lambda_init):
    del mix_ref
    lp = lam_ref[...]
    lam = (jnp.exp(jnp.sum(lp[0:1] * lp[1:2], axis=-1, keepdims=True))
           - jnp.exp(jnp.sum(lp[2:3] * lp[3:4], axis=-1, keepdims=True)) + lambda_init)
    k1t = k1t_ref[0, 0]
    k2t = k2t_ref[0, 0]
    v = v_ref[0]
    for c in range(q_blocks):
        i = pl.program_id(2) * q_blocks + c
        rows = slice(c * Q_BLOCK, (c + 1) * Q_BLOCK)
        bias = jnp.concatenate(
            [t5_ref[0, jnp.clip(kj - i + T5_BAND0 + 1, T5_FAR_LO, T5_FAR_HI)]
             for kj in range(n_blocks)], axis=1)
        e1, l1 = _softmax_parts(_dot(q1_ref[0, rows, :], k1t) + bias)
        e2, l2 = _softmax_parts(_dot(q2_ref[0, rows, :], k2t) + bias)
        ev = _dot(jnp.concatenate([e1.astype(BF16), e2.astype(BF16)], axis=0), v)
        od = ev[:Q_BLOCK] * (1.0 / l1) - ev[Q_BLOCK:] * (lam / l2)
        o_ref[0, rows, :] = (_rms(od, g_ref[...]) * (1.0 - lambda_init)).astype(o_ref.dtype)


def _attn_d(proj, kt, t5_blocks, lam_params, subln, lambda_init, mix, q_blocks=8):
    b, s, _ = proj.shape
    nb = s // Q_BLOCK
    hd = HEAD_DIM
    tq = q_blocks * Q_BLOCK
    return pl.pallas_call(
        functools.partial(_attn_d_kernel, n_blocks=nb, q_blocks=q_blocks, lambda_init=lambda_init),
        grid=(b, DF_HEADS, nb // q_blocks),
        in_specs=[
            pl.BlockSpec((1, tq, hd), lambda bb, h, i: (bb, i, OFF_DQ // hd + h)),
            pl.BlockSpec((1, tq, hd), lambda bb, h, i: (bb, i, OFF_DQ // hd + DF_HEADS + h)),
            pl.BlockSpec((1, 1, hd, s), lambda bb, h, i: (bb, KT_D + h, 0, 0)),
            pl.BlockSpec((1, 1, hd, s), lambda bb, h, i: (bb, KT_D + DF_HEADS + h, 0, 0)),
            pl.BlockSpec((1, s, DF_V_DIM), lambda bb, h, i: (bb, 0, OFF_DV // DF_V_DIM + h)),
            pl.BlockSpec((1, T5_KINDS, Q_BLOCK, Q_BLOCK), lambda bb, h, i: (SW_HEADS + h, 0, 0, 0)),
            pl.BlockSpec((4, hd), lambda bb, h, i: (0, 0)),
            pl.BlockSpec((1, DF_V_DIM), lambda bb, h, i: (0, 0)),
            pl.BlockSpec(memory_space=pl.ANY),
        ],
        out_specs=pl.BlockSpec((1, tq, DF_V_DIM), lambda bb, h, i: (bb, i, MIX_D // DF_V_DIM + h)),
        out_shape=jax.ShapeDtypeStruct(mix.shape, mix.dtype),
        input_output_aliases={8: 0},
        compiler_params=_cparams(("parallel", "parallel", "arbitrary")),
        name="attn_d",
    )(proj, proj, kt, kt, proj, t5_blocks, lam_params, subln.reshape(1, -1), mix)


def _rope_tables(s):
    pos = jnp.arange(s, dtype=jnp.int32)
    row = (pos // GRID_W).astype(F32)
    col = (pos % GRID_W).astype(F32)
    n_pairs = HEAD_DIM // 4
    inv = ROPE_THETA ** (-jnp.arange(n_pairs, dtype=F32) / n_pairs)
    ang = jnp.concatenate([row[:, None] * inv, col[:, None] * inv], axis=-1)
    cos = jnp.repeat(jnp.cos(ang), 2, axis=-1)
    sin = jnp.repeat(jnp.sin(ang), 2, axis=-1)
    sign = jnp.tile(jnp.asarray([-1.0, 1.0], F32), HEAD_DIM // 2)
    return cos, sin * sign


def _q_col_scale():
    col = jnp.arange(D_IN, dtype=jnp.int32)
    is_q = (((col >= OFF_AQ) & (col < OFF_AK)) | ((col >= OFF_CQ) & (col < OFF_CK))
            | ((col >= OFF_DQ) & (col < OFF_DK)))
    return jnp.where(is_q, LOGIT_SCALE, 1.0).astype(F32).reshape(1, D_IN)


def kernel(x, ln_attn_pre, ln_attn_post, ln_mlp_pre, ln_mlp_post, w_in, w_out, na_rpb,
           ax_q_norm, ax_k_norm, sw_sink, df_lambda, df_subln, t5_table, w_mlp_in, w_mlp_out):
    b, s, d = x.shape
    depth = w_in.shape[0]
    t = b * s
    cos, sin_signed = _rope_tables(s)
    col_scale = _q_col_scale()
    t5_blocks = _t5_blocks(t5_table)
    xf = x.reshape(t, d)
    h = _norm_cast(xf, ln_attn_pre[0])
    for l in range(depth):
        proj = _matmul(h, w_in, l, BF16, col_scale=col_scale, name="in_proj").reshape(b, s, D_IN)
        qn_b, kt = _prep(proj, cos, sin_signed, ax_q_norm[l], ax_k_norm[l])
        mix = _attn_a(proj, kt, _na_table(na_rpb[l]))
        mix = _attn_b(qn_b, kt, proj, mix)
        mix = _attn_c(proj, kt, t5_blocks, sw_sink[l], mix)
        lambda_init = 0.8 - 0.6 * math.exp(-0.3 * l)
        mix = _attn_d(proj, kt, t5_blocks, df_lambda[l], df_subln[l], lambda_init, mix)
        y = _matmul(mix.reshape(t, d), w_out, l, BF16, name="out_proj")
        xf, h = _resid_norm(xf, y, ln_attn_post[l], ln_mlp_pre[l])
        u = _matmul(h, w_mlp_in, l, BF16, act=_relu2, name="mlp_in")
        y = _matmul_ksplit(u, w_mlp_out, l, BF16, name="mlp_out")
        g_next = ln_attn_pre[l + 1] if l + 1 < depth else None
        xf, h = _resid_norm(xf, y, ln_mlp_post[l], g_next)
    return xf.reshape(b, s, d)
```

```python
import functools
import math

import jax
import jax.numpy as jnp
from jax import lax
from jax.experimental import pallas as pl
from jax.experimental.pallas import tpu as pltpu

F32 = jnp.float32
BF16 = jnp.bfloat16

D_MODEL = 4096
GRID_W = 64
HEAD_DIM = 128
Q_BLOCK = 128
GROUP_W = D_MODEL // 4
NA_HEADS = GROUP_W // HEAD_DIM
NA_WIN_ROWS = 8
NA_WIN_COLS = 16
AX_HEADS = GROUP_W // HEAD_DIM
AX_KV_HEADS = AX_HEADS // 4
ROPE_THETA = 10000.0
SW_HEADS = GROUP_W // HEAD_DIM
SW_KV_HEADS = SW_HEADS // 4
SW_WINDOW = 128
DF_V_DIM = 2 * HEAD_DIM
DF_HEADS = GROUP_W // DF_V_DIM
T5_BUCKETS = 32
T5_MAX_DIST = 128
T5_HEADS = SW_HEADS + DF_HEADS
EPS = 1e-6
SCALE = HEAD_DIM ** -0.5
LOG2E = math.log2(math.e)
LOGIT_SCALE = SCALE * LOG2E

OFF_AQ, OFF_AK, OFF_AV = 0, 1024, 2048
OFF_BQ, OFF_BK, OFF_BV = 3072, 4096, 4352
OFF_CQ, OFF_CK, OFF_CV = 4608, 5632, 5888
OFF_DQ, OFF_DK, OFF_DV = 6144, 7168, 8192
D_IN = 9216
MIX_A, MIX_B, MIX_C, MIX_D = 0, GROUP_W, 2 * GROUP_W, 3 * GROUP_W

NA_SPAN_BLOCKS = 5
NA_REL_BLOCKS = 9
NA_RPB_ROWS = 2 * NA_WIN_ROWS - 1
NA_RPB_COLS = 2 * NA_WIN_COLS - 1
NA_RPB_ROWS_PADDED = 16

T5_FAR_LO, T5_BAND0, T5_FAR_HI, T5_MASKED, T5_KINDS = 0, 1, 4, 5, 6

VMEM_LIMIT = 60 * 1024 * 1024
NEG_INF = float("-inf")


def _cparams(sem):
    return pltpu.CompilerParams(dimension_semantics=sem, vmem_limit_bytes=VMEM_LIMIT)


def _dot(a, b):
    return jnp.dot(a, b, preferred_element_type=F32)


def _dot_nt(a, b):
    return lax.dot_general(a, b, (((1,), (1,)), ((), ())), preferred_element_type=F32)


def _rms(x, g):
    return x * lax.rsqrt(jnp.mean(x * x, axis=-1, keepdims=True) + EPS) * g


def _softmax_parts(s, extra=None):
    m = jnp.max(s, axis=-1, keepdims=True)
    if extra is not None:
        m = jnp.maximum(m, extra)
    e = jnp.exp2(s - m)
    l = jnp.sum(e, axis=-1, keepdims=True)
    if extra is not None:
        l = l + jnp.exp2(extra - m)
    return e, l


def _norm_cast_kernel(x_ref, g_ref, h_ref):
    h_ref[...] = _rms(x_ref[...], g_ref[...]).astype(h_ref.dtype)


def _norm_cast(x, g, rows=256):
    t, d = x.shape
    return pl.pallas_call(
        _norm_cast_kernel,
        grid=(t // rows,),
        in_specs=[pl.BlockSpec((rows, d), lambda i: (i, 0)),
                  pl.BlockSpec((1, d), lambda i: (0, 0))],
        out_specs=pl.BlockSpec((rows, d), lambda i: (i, 0)),
        out_shape=jax.ShapeDtypeStruct((t, d), BF16),
        compiler_params=_cparams(("parallel",)),
        name="norm_cast",
    )(x, g.reshape(1, d))


def _resid_norm_kernel(x_ref, y_ref, gp_ref, gn_ref, xo_ref, h_ref):
    xn = x_ref[...] + _rms(y_ref[...].astype(F32), gp_ref[...])
    xo_ref[...] = xn
    h_ref[...] = _rms(xn, gn_ref[...]).astype(h_ref.dtype)


def _resid_kernel(x_ref, y_ref, gp_ref, xo_ref):
    xo_ref[...] = x_ref[...] + _rms(y_ref[...].astype(F32), gp_ref[...])


def _resid_norm(x, y, g_post, g_next, rows=256):
    t, d = x.shape
    row_spec = pl.BlockSpec((rows, d), lambda i: (i, 0))
    g_spec = pl.BlockSpec((1, d), lambda i: (0, 0))
    if g_next is None:
        return pl.pallas_call(
            _resid_kernel, grid=(t // rows,),
            in_specs=[row_spec, row_spec, g_spec], out_specs=row_spec,
            out_shape=jax.ShapeDtypeStruct((t, d), F32),
            compiler_params=_cparams(("parallel",)), name="resid",
        )(x, y, g_post.reshape(1, d)), None
    return pl.pallas_call(
        _resid_norm_kernel, grid=(t // rows,),
        in_specs=[row_spec, row_spec, g_spec, g_spec], out_specs=[row_spec, row_spec],
        out_shape=[jax.ShapeDtypeStruct((t, d), F32), jax.ShapeDtypeStruct((t, d), BF16)],
        compiler_params=_cparams(("parallel",)), name="resid_norm",
    )(x, y, g_post.reshape(1, d), g_next.reshape(1, d))


def _relu2(r):
    r = jnp.maximum(r, 0.0)
    return r * r


def _mm_kernel(a_ref, w_ref, *rest, act, col_scaled):
    o_ref = rest[-1]
    r = _dot(a_ref[...], w_ref[...].astype(BF16))
    if act is not None:
        r = act(r)
    if col_scaled:
        r = r * rest[0][...]
    o_ref[...] = r.astype(o_ref.dtype)


def _matmul(a, w, layer, out_dtype, *, tm=2048, tn=512, act=None, col_scale=None, name="matmul"):
    m, kdim = a.shape
    _, _, n = w.shape
    in_specs = [pl.BlockSpec((tm, kdim), lambda i, j: (i, 0)),
                pl.BlockSpec((None, kdim, tn), lambda i, j: (layer, 0, j))]
    args = [a, w]
    if col_scale is not None:
        in_specs.append(pl.BlockSpec((1, tn), lambda i, j: (0, j)))
        args.append(col_scale)
    return pl.pallas_call(
        functools.partial(_mm_kernel, act=act, col_scaled=col_scale is not None),
        grid=(m // tm, n // tn),
        in_specs=in_specs,
        out_specs=pl.BlockSpec((tm, tn), lambda i, j: (i, j)),
        out_shape=jax.ShapeDtypeStruct((m, n), out_dtype),
        compiler_params=_cparams(("parallel", "parallel")),
        name=name,
    )(*args)


def _mm_ksplit_kernel(a_ref, w_ref, o_ref, acc_ref):
    k = pl.program_id(2)

    @pl.when(k == 0)
    def _():
        acc_ref[...] = jnp.zeros(acc_ref.shape, acc_ref.dtype)

    acc_ref[...] += _dot(a_ref[...], w_ref[...].astype(BF16))

    @pl.when(k == pl.num_programs(2) - 1)
    def _():
        o_ref[...] = acc_ref[...].astype(o_ref.dtype)


def _matmul_ksplit(a, w, layer, out_dtype, *, tm=2048, tn=1024, tk=2048, name="matmul_ksplit"):
    m, kdim = a.shape
    _, _, n = w.shape
    return pl.pallas_call(
        _mm_ksplit_kernel,
        grid=(m // tm, n // tn, kdim // tk),
        in_specs=[pl.BlockSpec((tm, tk), lambda i, j, k: (i, k)),
                  pl.BlockSpec((None, tk, tn), lambda i, j, k: (layer, k, j))],
        out_specs=pl.BlockSpec((tm, tn), lambda i, j, k: (i, j)),
        out_shape=jax.ShapeDtypeStruct((m, n), out_dtype),
        scratch_shapes=[pltpu.VMEM((tm, tn), F32)],
        compiler_params=_cparams(("parallel", "parallel", "arbitrary")),
        name=name,
    )(a, w)


def _t5_bucket(rel):
    nb = T5_BUCKETS // 2
    max_exact = nb // 2
    base = jnp.where(rel > 0, nb, 0)
    n = jnp.abs(rel)
    n_f = jnp.maximum(n, 1).astype(F32)
    large = max_exact + (jnp.log(n_f / max_exact) / math.log(T5_MAX_DIST / max_exact)
                         * (nb - max_exact)).astype(jnp.int32)
    large = jnp.minimum(large, nb - 1)
    return base + jnp.where(n < max_exact, n, large)


def _t5_blocks_kernel(idx_ref, tab_ref, o_ref):
    h = pl.program_id(0)
    shape = (Q_BLOCK, Q_BLOCK)
    t = lax.broadcasted_iota(jnp.int32, shape, 0)
    u = lax.broadcasted_iota(jnp.int32, shape, 1)
    o_ref[0, T5_FAR_LO] = jnp.full(shape, tab_ref[T5_BUCKETS // 2 - 1, h] * LOG2E, F32)
    o_ref[0, T5_FAR_HI] = jnp.full(shape, tab_ref[T5_BUCKETS - 1, h] * LOG2E, F32)
    o_ref[0, T5_MASKED] = jnp.full(shape, NEG_INF, F32)
    for j in range(3):
        idx = idx_ref[j]
        diag = jnp.zeros(idx.shape, F32)
        for b in range(T5_BUCKETS):
            diag = jnp.where(idx == b, tab_ref[b, h], diag)
        upper = pltpu.roll(jnp.broadcast_to(diag[0:1], shape), 0, 1, stride=1, stride_axis=0)
        lower = pltpu.roll(jnp.broadcast_to(diag[1:2], shape), 0, 1, stride=1, stride_axis=0)
        band = jnp.where(u >= t, upper, lower)
        rel = u + (j - 1) * Q_BLOCK - t
        allowed = (jnp.abs(rel) <= SW_WINDOW) | (h >= SW_HEADS)
        o_ref[0, T5_BAND0 + j] = jnp.where(allowed, band, NEG_INF) * LOG2E


def _t5_blocks(t5_table):
    k = jnp.arange(Q_BLOCK, dtype=jnp.int32)
    block_off = (jnp.arange(3, dtype=jnp.int32) - 1) * Q_BLOCK
    rel = block_off[:, None, None] + jnp.stack([k, k - Q_BLOCK])[None]
    idx = _t5_bucket(rel).astype(jnp.int32)
    return pl.pallas_call(
        _t5_blocks_kernel,
        grid=(T5_HEADS,),
        in_specs=[pl.BlockSpec((3, 2, Q_BLOCK), lambda h: (0, 0, 0)),
                  pl.BlockSpec(memory_space=pltpu.SMEM)],
        out_specs=pl.BlockSpec((1, T5_KINDS, Q_BLOCK, Q_BLOCK), lambda h: (h, 0, 0, 0)),
        out_shape=jax.ShapeDtypeStruct((T5_HEADS, T5_KINDS, Q_BLOCK, Q_BLOCK), F32),
        compiler_params=_cparams(("arbitrary",)),
        name="t5_blocks",
    )(idx, t5_table)


def _na_table_kernel(rpb_ref, o_ref):
    shape = (GRID_W, Q_BLOCK)
    c = lax.broadcasted_iota(jnp.int32, shape, 0)
    u = lax.broadcasted_iota(jnp.int32, shape, 1)
    k_hi = u >= GRID_W
    kc = jnp.bitwise_and(u, GRID_W - 1)
    cs = jnp.clip(c - NA_WIN_COLS // 2, 0, GRID_W - NA_WIN_COLS)
    col_ok = (kc >= cs) & (kc < cs + NA_WIN_COLS)

    def toeplitz(dr, lane_off):
        if abs(dr) > NA_WIN_ROWS - 1:
            return jnp.full(shape, NEG_INF, F32)
        row = dr + NA_WIN_ROWS - 1
        vec = jnp.broadcast_to(rpb_ref[0, row:row + 1, :], shape)
        shift = (lane_off - (NA_WIN_COLS - 1)) % Q_BLOCK
        return pltpu.roll(vec, shift, 1, stride=1, stride_axis=0)

    for d in range(NA_REL_BLOCKS):
        dblk = d - NA_REL_BLOCKS // 2
        halves = []
        for a in range(2):
            even = toeplitz(2 * dblk - a, 0)
            odd = toeplitz(2 * dblk + 1 - a, GRID_W)
            halves.append(jnp.where(col_ok, jnp.where(k_hi, odd, even), NEG_INF))
        o_ref[0, d] = jnp.concatenate(halves, axis=0) * LOG2E


def _na_table(rpb):
    heads, n_rows, n_cols = rpb.shape
    padded = jnp.pad(rpb, ((0, 0), (0, NA_RPB_ROWS_PADDED - n_rows), (0, Q_BLOCK - n_cols)))
    return pl.pallas_call(
        _na_table_kernel,
        grid=(heads,),
        in_specs=[pl.BlockSpec((1, NA_RPB_ROWS_PADDED, Q_BLOCK), lambda h: (h, 0, 0))],
        out_specs=pl.BlockSpec((1, NA_REL_BLOCKS, Q_BLOCK, Q_BLOCK), lambda h: (h, 0, 0, 0)),
        out_shape=jax.ShapeDtypeStruct((heads, NA_REL_BLOCKS, Q_BLOCK, Q_BLOCK), F32),
        compiler_params=_cparams(("arbitrary",)),
        name="na_table",
    )(padded)


def _rope(x, cos, sin_signed):
    lane = lax.broadcasted_iota(jnp.int32, x.shape, 1)
    even = jnp.bitwise_and(lane, 1) == 0
    swapped = jnp.where(even, pltpu.roll(x, HEAD_DIM - 1, 1), pltpu.roll(x, 1, 1))
    return x * cos + swapped * sin_signed


def _prep_kernel(bq_ref, bk_ref, cos_ref, sin_ref, qg_ref, kg_ref, qo_ref, ko_ref):
    cos = cos_ref[...]
    sin = sin_ref[...]
    for h in range(AX_HEADS):
        cols = slice(h * HEAD_DIM, (h + 1) * HEAD_DIM)
        qh = _rope(_rms(bq_ref[0, :, cols].astype(F32), qg_ref[...]), cos, sin)
        qo_ref[0, :, cols] = (qh * LOGIT_SCALE).astype(qo_ref.dtype)
    for h in range(AX_KV_HEADS):
        cols = slice(h * HEAD_DIM, (h + 1) * HEAD_DIM)
        kh = _rope(_rms(bk_ref[0, :, cols].astype(F32), kg_ref[...]), cos, sin)
        ko_ref[0, :, cols] = kh.astype(ko_ref.dtype)


def _prep(proj, cos, sin_signed, q_gain, k_gain, rows=512):
    b, s, _ = proj.shape
    kw = AX_KV_HEADS * HEAD_DIM
    return pl.pallas_call(
        _prep_kernel,
        grid=(b, s // rows),
        in_specs=[
            pl.BlockSpec((1, rows, GROUP_W), lambda bb, r: (bb, r, OFF_BQ // GROUP_W)),
            pl.BlockSpec((1, rows, kw), lambda bb, r: (bb, r, OFF_BK // kw)),
            pl.BlockSpec((rows, HEAD_DIM), lambda bb, r: (r, 0)),
            pl.BlockSpec((rows, HEAD_DIM), lambda bb, r: (r, 0)),
            pl.BlockSpec((1, HEAD_DIM), lambda bb, r: (0, 0)),
            pl.BlockSpec((1, HEAD_DIM), lambda bb, r: (0, 0)),
        ],
        out_specs=[pl.BlockSpec((1, rows, GROUP_W), lambda bb, r: (bb, r, 0)),
                   pl.BlockSpec((1, rows, kw), lambda bb, r: (bb, r, 0))],
        out_shape=[jax.ShapeDtypeStruct((b, s, GROUP_W), BF16),
                   jax.ShapeDtypeStruct((b, s, kw), BF16)],
        compiler_params=_cparams(("parallel", "parallel")),
        name="prep",
    )(proj, proj, cos, sin_signed, q_gain.reshape(1, -1), k_gain.reshape(1, -1))


def _attn_a_kernel(q_ref, k_ref, v_ref, tb_ref, o_ref, *, n_blocks, q_blocks):
    span = NA_SPAN_BLOCKS * Q_BLOCK
    grid_rows = 2 * n_blocks
    key_half = jnp.where(lax.broadcasted_iota(jnp.int32, (1, Q_BLOCK), 1) >= GRID_W, 1, 0)
    for c in range(q_blocks):
        i = pl.program_id(2) * q_blocks + c
        sb = jnp.clip(i - 2, 0, n_blocks - NA_SPAN_BLOCKS)
        start = pl.multiple_of(sb * Q_BLOCK, Q_BLOCK)
        rows = slice(c * Q_BLOCK, (c + 1) * Q_BLOCK)
        s = _dot_nt(q_ref[0, rows, :], k_ref[0, pl.ds(start, span), :])
        pieces = []
        for j in range(NA_SPAN_BLOCKS):
            blk = tb_ref[0, sb + j - i + NA_REL_BLOCKS // 2]
            key_row = 2 * (sb + j) + key_half
            halves = []
            for a in range(2):
                first = jnp.clip(2 * i + a - NA_WIN_ROWS // 2, 0, grid_rows - NA_WIN_ROWS)
                row_ok = (key_row >= first) & (key_row < first + NA_WIN_ROWS)
                halves.append(blk[a * GRID_W:(a + 1) * GRID_W] + jnp.where(row_ok, 0.0, NEG_INF))
            pieces.append(jnp.concatenate(halves, axis=0))
        e, l = _softmax_parts(s + jnp.concatenate(pieces, axis=1))
        o = _dot(e.astype(BF16), v_ref[0, pl.ds(start, span), :]) * (1.0 / l)
        o_ref[0, rows, :] = o.astype(o_ref.dtype)


def _attn_a(proj, table, q_blocks=16):
    b, s, _ = proj.shape
    nb = s // Q_BLOCK
    hd = HEAD_DIM
    tq = q_blocks * Q_BLOCK
    return pl.pallas_call(
        functools.partial(_attn_a_kernel, n_blocks=nb, q_blocks=q_blocks),
        grid=(NA_HEADS, b, nb // q_blocks),
        in_specs=[
            pl.BlockSpec((1, tq, hd), lambda h, bb, i: (bb, i, OFF_AQ // hd + h)),
            pl.BlockSpec((1, s, hd), lambda h, bb, i: (bb, 0, OFF_AK // hd + h)),
            pl.BlockSpec((1, s, hd), lambda h, bb, i: (bb, 0, OFF_AV // hd + h)),
            pl.BlockSpec((1, NA_REL_BLOCKS, Q_BLOCK, Q_BLOCK), lambda h, bb, i: (h, 0, 0, 0)),
        ],
        out_specs=pl.BlockSpec((1, tq, hd), lambda h, bb, i: (bb, i, MIX_A // hd + h)),
        out_shape=jax.ShapeDtypeStruct((b, s, D_MODEL), BF16),
        compiler_params=_cparams(("parallel", "parallel", "arbitrary")),
        name="attn_a",
    )(proj, proj, proj, table)


def _attn_b_kernel(q_ref, k_ref, v_ref, mix_ref, o_ref):
    del mix_ref
    k = k_ref[0]
    v = v_ref[0]
    for g in range(AX_HEADS // AX_KV_HEADS):
        cols = slice(g * HEAD_DIM, (g + 1) * HEAD_DIM)
        e, l = _softmax_parts(_dot_nt(q_ref[0, :, cols], k))
        o = _dot(e.astype(BF16), v) * (1.0 / l)
        o_ref[0, :, cols] = o.astype(o_ref.dtype)


def _attn_b(qn, kn, proj, mix, tq=1024):
    b, s, _ = proj.shape
    qw = (AX_HEADS // AX_KV_HEADS) * HEAD_DIM
    return pl.pallas_call(
        _attn_b_kernel,
        grid=(b, AX_KV_HEADS, s // tq),
        in_specs=[
            pl.BlockSpec((1, tq, qw), lambda bb, kv, i: (bb, i, kv)),
            pl.BlockSpec((1, s, HEAD_DIM), lambda bb, kv, i: (bb, 0, kv)),
            pl.BlockSpec((1, s, HEAD_DIM), lambda bb, kv, i: (bb, 0, OFF_BV // HEAD_DIM + kv)),
            pl.BlockSpec(memory_space=pl.ANY),
        ],
        out_specs=pl.BlockSpec((1, tq, qw), lambda bb, kv, i: (bb, i, MIX_B // qw + kv)),
        out_shape=jax.ShapeDtypeStruct(mix.shape, mix.dtype),
        input_output_aliases={3: 0},
        compiler_params=_cparams(("parallel", "parallel", "arbitrary")),
        name="attn_b",
    )(qn, kn, proj, mix)


def _attn_c_kernel(q_ref, k_ref, v_ref, t5_ref, sink_ref, mix_ref, o_ref, *, n_blocks, q_blocks):
    del mix_ref
    kv = pl.program_id(1)
    groups = SW_HEADS // SW_KV_HEADS
    sink = jnp.concatenate(
        [jnp.full((Q_BLOCK, 1), sink_ref[kv * groups + g] * LOG2E, F32) for g in range(groups)],
        axis=0)
    for c in range(q_blocks):
        i = pl.program_id(2) * q_blocks + c
        rows = slice(c * Q_BLOCK, (c + 1) * Q_BLOCK)
        blocks = (jnp.maximum(i - 1, 0), i, jnp.minimum(i + 1, n_blocks - 1))
        kinds = (jnp.where(i > 0, T5_BAND0, T5_MASKED), T5_BAND0 + 1,
                 jnp.where(i < n_blocks - 1, T5_BAND0 + 2, T5_MASKED))
        starts = [pl.multiple_of(blk * Q_BLOCK, Q_BLOCK) for blk in blocks]
        q = jnp.concatenate(
            [q_ref[0, rows, g * HEAD_DIM:(g + 1) * HEAD_DIM] for g in range(groups)], axis=0)
        s = jnp.concatenate(
            [_dot_nt(q, k_ref[0, pl.ds(start, Q_BLOCK), :])
             + jnp.concatenate([t5_ref[g, kind] for g in range(groups)], axis=0)
             for start, kind in zip(starts, kinds)], axis=1)
        e, l = _softmax_parts(s, extra=sink)
        v = jnp.concatenate([v_ref[0, pl.ds(start, Q_BLOCK), :] for start in starts], axis=0)
        o = _dot(e.astype(BF16), v) * (1.0 / l)
        for g in range(groups):
            o_ref[0, rows, g * HEAD_DIM:(g + 1) * HEAD_DIM] = (
                o[g * Q_BLOCK:(g + 1) * Q_BLOCK].astype(o_ref.dtype))


def _attn_c(proj, t5_blocks, sink, mix, q_blocks=8):
    b, s, _ = proj.shape
    nb = s // Q_BLOCK
    groups = SW_HEADS // SW_KV_HEADS
    qw = groups * HEAD_DIM
    hd = HEAD_DIM
    tq = q_blocks * Q_BLOCK
    return pl.pallas_call(
        functools.partial(_attn_c_kernel, n_blocks=nb, q_blocks=q_blocks),
        grid=(b, SW_KV_HEADS, nb // q_blocks),
        in_specs=[
            pl.BlockSpec((1, tq, qw), lambda bb, kv, i: (bb, i, OFF_CQ // qw + kv)),
            pl.BlockSpec((1, s, hd), lambda bb, kv, i: (bb, 0, OFF_CK // hd + kv)),
            pl.BlockSpec((1, s, hd), lambda bb, kv, i: (bb, 0, OFF_CV // hd + kv)),
            pl.BlockSpec((groups, T5_KINDS, Q_BLOCK, Q_BLOCK), lambda bb, kv, i: (kv, 0, 0, 0)),
            pl.BlockSpec(memory_space=pltpu.SMEM),
            pl.BlockSpec(memory_space=pl.ANY),
        ],
        out_specs=pl.BlockSpec((1, tq, qw), lambda bb, kv, i: (bb, i, MIX_C // qw + kv)),
        out_shape=jax.ShapeDtypeStruct(mix.shape, mix.dtype),
        input_output_aliases={5: 0},
        compiler_params=_cparams(("parallel", "parallel", "arbitrary")),
        name="attn_c",
    )(proj, proj, proj, t5_blocks, sink, mix)


def _attn_d_kernel(q1_ref, q2_ref, k1_ref, k2_ref, v_ref, t5_ref, lam_ref, g_ref, mix_ref, o_ref,
                   *, n_blocks, q_blocks, lambda_init):
    del mix_ref
    lp = lam_ref[...]
    lam = (jnp.exp(jnp.sum(lp[0:1] * lp[1:2], axis=-1, keepdims=True))
           - jnp.exp(jnp.sum(lp[2:3] * lp[3:4], axis=-1, keepdims=True)) + lambda_init)
    k1 = k1_ref[0]
    k2 = k2_ref[0]
    v = v_ref[0]
    for c in range(q_blocks):
        i = pl.program_id(2) * q_blocks + c
        rows = slice(c * Q_BLOCK, (c + 1) * Q_BLOCK)
        bias = jnp.concatenate(
            [t5_ref[0, jnp.clip(kj - i + T5_BAND0 + 1, T5_FAR_LO, T5_FAR_HI)]
             for kj in range(n_blocks)], axis=1)
        e1, l1 = _softmax_parts(_dot_nt(q1_ref[0, rows, :], k1) + bias)
        e2, l2 = _softmax_parts(_dot_nt(q2_ref[0, rows, :], k2) + bias)
        ev = _dot(jnp.concatenate([e1.astype(BF16), e2.astype(BF16)], axis=0), v)
        od = ev[:Q_BLOCK] * (1.0 / l1) - ev[Q_BLOCK:] * (lam / l2)
        o_ref[0, rows, :] = (_rms(od, g_ref[...]) * (1.0 - lambda_init)).astype(o_ref.dtype)


def _attn_d(proj, t5_blocks, lam_params, subln, lambda_init, mix, q_blocks=16):
    b, s, _ = proj.shape
    nb = s // Q_BLOCK
    hd = HEAD_DIM
    tq = q_blocks * Q_BLOCK
    return pl.pallas_call(
        functools.partial(_attn_d_kernel, n_blocks=nb, q_blocks=q_blocks, lambda_init=lambda_init),
        grid=(b, DF_HEADS, nb // q_blocks),
        in_specs=[
            pl.BlockSpec((1, tq, hd), lambda bb, h, i: (bb, i, OFF_DQ // hd + h)),
            pl.BlockSpec((1, tq, hd), lambda bb, h, i: (bb, i, OFF_DQ // hd + DF_HEADS + h)),
            pl.BlockSpec((1, s, hd), lambda bb, h, i: (bb, 0, OFF_DK // hd + h)),
            pl.BlockSpec((1, s, hd), lambda bb, h, i: (bb, 0, OFF_DK // hd + DF_HEADS + h)),
            pl.BlockSpec((1, s, DF_V_DIM), lambda bb, h, i: (bb, 0, OFF_DV // DF_V_DIM + h)),
            pl.BlockSpec((1, T5_KINDS, Q_BLOCK, Q_BLOCK), lambda bb, h, i: (SW_HEADS + h, 0, 0, 0)),
            pl.BlockSpec((4, hd), lambda bb, h, i: (0, 0)),
            pl.BlockSpec((1, DF_V_DIM), lambda bb, h, i: (0, 0)),
            pl.BlockSpec(memory_space=pl.ANY),
        ],
        out_specs=pl.BlockSpec((1, tq, DF_V_DIM), lambda bb, h, i: (bb, i, MIX_D // DF_V_DIM + h)),
        out_shape=jax.ShapeDtypeStruct(mix.shape, mix.dtype),
        input_output_aliases={8: 0},
        compiler_params=_cparams(("parallel", "parallel", "arbitrary")),
        name="attn_d",
    )(proj, proj, proj, proj, proj, t5_blocks, lam_params, subln.reshape(1, -1), mix)


def _rope_tables(s):
    pos = jnp.arange(s, dtype=jnp.int32)
    row = (pos // GRID_W).astype(F32)
    col = (pos % GRID_W).astype(F32)
    n_pairs = HEAD_DIM // 4
    inv = ROPE_THETA ** (-jnp.arange(n_pairs, dtype=F32) / n_pairs)
    ang = jnp.concatenate([row[:, None] * inv, col[:, None] * inv], axis=-1)
    cos = jnp.repeat(jnp.cos(ang), 2, axis=-1)
    sin = jnp.repeat(jnp.sin(ang), 2, axis=-1)
    sign = jnp.tile(jnp.asarray([-1.0, 1.0], F32), HEAD_DIM // 2)
    return cos, sin * sign


def _q_col_scale():
    col = jnp.arange(D_IN, dtype=jnp.int32)
    is_q = (((col >= OFF_AQ) & (col < OFF_AK)) | ((col >= OFF_CQ) & (col < OFF_CK))
            | ((col >= OFF_DQ) & (col < OFF_DK)))
    return jnp.where(is_q, LOGIT_SCALE, 1.0).astype(F32).reshape(1, D_IN)


def kernel(x, ln_attn_pre, ln_attn_post, ln_mlp_pre, ln_mlp_post, w_in, w_out, na_rpb,
           ax_q_norm, ax_k_norm, sw_sink, df_lambda, df_subln, t5_table, w_mlp_in, w_mlp_out):
    b, s, d = x.shape
    depth = w_in.shape[0]
    t = b * s
    cos, sin_signed = _rope_tables(s)
    col_scale = _q_col_scale()
    t5_blocks = _t5_blocks(t5_table)
    xf = x.reshape(t, d)
    h = _norm_cast(xf, ln_attn_pre[0])
    for l in range(depth):
        proj = _matmul(h, w_in, l, BF16, col_scale=col_scale, name="in_proj").reshape(b, s, D_IN)
        qn_b, kn_b = _prep(proj, cos, sin_signed, ax_q_norm[l], ax_k_norm[l])
        mix = _attn_a(proj, _na_table(na_rpb[l]))
        mix = _attn_b(qn_b, kn_b, proj, mix)
        mix = _attn_c(proj, t5_blocks, sw_sink[l], mix)
        lambda_init = 0.8 - 0.6 * math.exp(-0.3 * l)
        mix = _attn_d(proj, t5_blocks, df_lambda[l], df_subln[l], lambda_init, mix)
        y = _matmul(mix.reshape(t, d), w_out, l, BF16, name="out_proj")
        xf, h = _resid_norm(xf, y, ln_attn_post[l], ln_mlp_pre[l])
        u = _matmul(h, w_mlp_in, l, BF16, act=_relu2, name="mlp_in")
        y = _matmul_ksplit(u, w_mlp_out, l, BF16, name="mlp_out")
        g_next = ln_attn_pre[l + 1] if l + 1 < depth else None
        xf, h = _resid_norm(xf, y, ln_mlp_post[l], g_next)
    return xf.reshape(b, s, d)
```

```python
import functools
import math

import jax
import jax.numpy as jnp
from jax import lax
from jax.experimental import pallas as pl
from jax.experimental.pallas import tpu as pltpu

F32 = jnp.float32
BF16 = jnp.bfloat16

D_MODEL = 4096
GRID_W = 64
HEAD_DIM = 128
Q_BLOCK = 128
GROUP_W = D_MODEL // 4
NA_HEADS = GROUP_W // HEAD_DIM
NA_WIN_ROWS = 8
NA_WIN_COLS = 16
AX_HEADS = GROUP_W // HEAD_DIM
AX_KV_HEADS = AX_HEADS // 4
ROPE_THETA = 10000.0
SW_HEADS = GROUP_W // HEAD_DIM
SW_KV_HEADS = SW_HEADS // 4
SW_WINDOW = 128
DF_V_DIM = 2 * HEAD_DIM
DF_HEADS = GROUP_W // DF_V_DIM
T5_BUCKETS = 32
T5_MAX_DIST = 128
T5_HEADS = SW_HEADS + DF_HEADS
EPS = 1e-6
SCALE = HEAD_DIM ** -0.5
LOG2E = math.log2(math.e)
LOGIT_SCALE = SCALE * LOG2E

_IN_WIDTHS = (
    GROUP_W, GROUP_W, GROUP_W,
    GROUP_W, AX_KV_HEADS * HEAD_DIM, AX_KV_HEADS * HEAD_DIM,
    GROUP_W, SW_KV_HEADS * HEAD_DIM, SW_KV_HEADS * HEAD_DIM,
    2 * DF_HEADS * HEAD_DIM, 2 * DF_HEADS * HEAD_DIM, DF_HEADS * DF_V_DIM,
)
(OFF_AQ, OFF_AK, OFF_AV, OFF_BQ, OFF_BK, OFF_BV,
 OFF_CQ, OFF_CK, OFF_CV, OFF_DQ, OFF_DK, OFF_DV) = (
    sum(_IN_WIDTHS[:n]) for n in range(len(_IN_WIDTHS)))
D_IN = sum(_IN_WIDTHS)
MIX_A, MIX_B, MIX_C, MIX_D = 0, GROUP_W, 2 * GROUP_W, 3 * GROUP_W

NA_SPAN_BLOCKS = 5
NA_REL_BLOCKS = 9
NA_GROUP = 4
NA_RPB_ROWS = 2 * NA_WIN_ROWS - 1
NA_RPB_COLS = 2 * NA_WIN_COLS - 1
NA_RPB_ROWS_PADDED = 16

T5_FAR_LO, T5_BAND0, T5_FAR_HI, T5_MASKED, T5_KINDS = 0, 1, 4, 5, 6

V7X_VMEM_BYTES = 64 * 1024 * 1024
VMEM_LIMIT = V7X_VMEM_BYTES - 4 * 1024 * 1024
NEG_INF = float("-inf")


def _cparams(sem):
    return pltpu.CompilerParams(dimension_semantics=sem, vmem_limit_bytes=VMEM_LIMIT)


def _dot(a, b):
    return jnp.dot(a, b, preferred_element_type=F32)


def _dot_nt(a, b):
    return lax.dot_general(a, b, (((1,), (1,)), ((), ())), preferred_element_type=F32)


def _rms(x, g):
    return x * lax.rsqrt(jnp.mean(x * x, axis=-1, keepdims=True) + EPS) * g


def _softmax_parts(s, extra=None):
    m = jnp.max(s, axis=-1, keepdims=True)
    if extra is not None:
        m = jnp.maximum(m, extra)
    e = jnp.exp2(s - m)
    l = jnp.sum(e, axis=-1, keepdims=True)
    if extra is not None:
        l = l + jnp.exp2(extra - m)
    return e, l


def _norm_cast_kernel(x_ref, g_ref, h_ref):
    h_ref[...] = _rms(x_ref[...], g_ref[...]).astype(h_ref.dtype)


def _norm_cast(x, g, rows=512):
    t, d = x.shape
    return pl.pallas_call(
        _norm_cast_kernel,
        grid=(t // rows,),
        in_specs=[pl.BlockSpec((rows, d), lambda i: (i, 0)),
                  pl.BlockSpec((1, d), lambda i: (0, 0))],
        out_specs=pl.BlockSpec((rows, d), lambda i: (i, 0)),
        out_shape=jax.ShapeDtypeStruct((t, d), BF16),
        compiler_params=_cparams(("parallel",)),
        name="norm_cast",
    )(x, g.reshape(1, d))


def _resid_norm_kernel(x_ref, y_ref, gp_ref, gn_ref, xo_ref, h_ref):
    xn = x_ref[...] + _rms(y_ref[...].astype(F32), gp_ref[...])
    xo_ref[...] = xn
    h_ref[...] = _rms(xn, gn_ref[...]).astype(h_ref.dtype)


def _resid_kernel(x_ref, y_ref, gp_ref, xo_ref):
    xo_ref[...] = x_ref[...] + _rms(y_ref[...].astype(F32), gp_ref[...])


def _resid_norm(x, y, g_post, g_next, rows=512):
    t, d = x.shape
    row_spec = pl.BlockSpec((rows, d), lambda i: (i, 0))
    g_spec = pl.BlockSpec((1, d), lambda i: (0, 0))
    if g_next is None:
        return pl.pallas_call(
            _resid_kernel, grid=(t // rows,),
            in_specs=[row_spec, row_spec, g_spec], out_specs=row_spec,
            out_shape=jax.ShapeDtypeStruct((t, d), F32),
            compiler_params=_cparams(("parallel",)), name="resid",
        )(x, y, g_post.reshape(1, d)), None
    return pl.pallas_call(
        _resid_norm_kernel, grid=(t // rows,),
        in_specs=[row_spec, row_spec, g_spec, g_spec], out_specs=[row_spec, row_spec],
        out_shape=[jax.ShapeDtypeStruct((t, d), F32), jax.ShapeDtypeStruct((t, d), BF16)],
        compiler_params=_cparams(("parallel",)), name="resid_norm",
    )(x, y, g_post.reshape(1, d), g_next.reshape(1, d))


def _relu2(r):
    r = jnp.maximum(r, 0.0)
    return r * r


def _mm_kernel(a_ref, w_ref, *rest, act, col_scaled):
    o_ref = rest[-1]
    r = _dot(a_ref[...], w_ref[...].astype(BF16))
    if act is not None:
        r = act(r)
    if col_scaled:
        r = r * rest[0][...]
    o_ref[...] = r.astype(o_ref.dtype)


def _matmul(a, w, layer, out_dtype, *, tm=2048, tn=512, act=None, col_scale=None, name="matmul"):
    m, kdim = a.shape
    _, _, n = w.shape
    in_specs = [pl.BlockSpec((tm, kdim), lambda i, j: (i, 0)),
                pl.BlockSpec((None, kdim, tn), lambda i, j: (layer, 0, j))]
    args = [a, w]
    if col_scale is not None:
        in_specs.append(pl.BlockSpec((1, tn), lambda i, j: (0, j)))
        args.append(col_scale)
    return pl.pallas_call(
        functools.partial(_mm_kernel, act=act, col_scaled=col_scale is not None),
        grid=(m // tm, n // tn),
        in_specs=in_specs,
        out_specs=pl.BlockSpec((tm, tn), lambda i, j: (i, j)),
        out_shape=jax.ShapeDtypeStruct((m, n), out_dtype),
        compiler_params=_cparams(("parallel", "parallel")),
        name=name,
    )(*args)


def _mm_ksplit_kernel(a_ref, w_ref, o_ref, acc_ref):
    k = pl.program_id(2)

    @pl.when(k == 0)
    def _():
        acc_ref[...] = jnp.zeros(acc_ref.shape, acc_ref.dtype)

    acc_ref[...] += _dot(a_ref[...], w_ref[...].astype(BF16))

    @pl.when(k == pl.num_programs(2) - 1)
    def _():
        o_ref[...] = acc_ref[...].astype(o_ref.dtype)


def _matmul_ksplit(a, w, layer, out_dtype, *, tm=2048, tn=1024, tk=2048, name="matmul_ksplit"):
    m, kdim = a.shape
    _, _, n = w.shape
    return pl.pallas_call(
        _mm_ksplit_kernel,
        grid=(m // tm, n // tn, kdim // tk),
        in_specs=[pl.BlockSpec((tm, tk), lambda i, j, k: (i, k)),
                  pl.BlockSpec((None, tk, tn), lambda i, j, k: (layer, k, j))],
        out_specs=pl.BlockSpec((tm, tn), lambda i, j, k: (i, j)),
        out_shape=jax.ShapeDtypeStruct((m, n), out_dtype),
        scratch_shapes=[pltpu.VMEM((tm, tn), F32)],
        compiler_params=_cparams(("parallel", "parallel", "arbitrary")),
        name=name,
    )(a, w)


def _t5_bucket(rel):
    nb = T5_BUCKETS // 2
    max_exact = nb // 2
    base = jnp.where(rel > 0, nb, 0)
    n = jnp.abs(rel)
    n_f = jnp.maximum(n, 1).astype(F32)
    large = max_exact + (jnp.log(n_f / max_exact) / math.log(T5_MAX_DIST / max_exact)
                         * (nb - max_exact)).astype(jnp.int32)
    large = jnp.minimum(large, nb - 1)
    return base + jnp.where(n < max_exact, n, large)


def _t5_blocks_kernel(idx_ref, tab_ref, o_ref):
    h = pl.program_id(0)
    shape = (Q_BLOCK, Q_BLOCK)
    t = lax.broadcasted_iota(jnp.int32, shape, 0)
    u = lax.broadcasted_iota(jnp.int32, shape, 1)
    o_ref[0, T5_FAR_LO] = jnp.full(shape, tab_ref[T5_BUCKETS // 2 - 1, h] * LOG2E, F32)
    o_ref[0, T5_FAR_HI] = jnp.full(shape, tab_ref[T5_BUCKETS - 1, h] * LOG2E, F32)
    o_ref[0, T5_MASKED] = jnp.full(shape, NEG_INF, F32)
    for j in range(3):
        idx = idx_ref[j]
        diag = jnp.zeros(idx.shape, F32)
        for b in range(T5_BUCKETS):
            diag = jnp.where(idx == b, tab_ref[b, h], diag)
        upper = pltpu.roll(jnp.broadcast_to(diag[0:1], shape), 0, 1, stride=1, stride_axis=0)
        lower = pltpu.roll(jnp.broadcast_to(diag[1:2], shape), 0, 1, stride=1, stride_axis=0)
        band = jnp.where(u >= t, upper, lower)
        rel = u + (j - 1) * Q_BLOCK - t
        allowed = (jnp.abs(rel) <= SW_WINDOW) | (h >= SW_HEADS)
        o_ref[0, T5_BAND0 + j] = jnp.where(allowed, band, NEG_INF) * LOG2E


def _t5_blocks(t5_table):
    k = jnp.arange(Q_BLOCK, dtype=jnp.int32)
    block_off = (jnp.arange(3, dtype=jnp.int32) - 1) * Q_BLOCK
    rel = block_off[:, None, None] + jnp.stack([k, k - Q_BLOCK])[None]
    idx = _t5_bucket(rel).astype(jnp.int32)
    return pl.pallas_call(
        _t5_blocks_kernel,
        grid=(T5_HEADS,),
        in_specs=[pl.BlockSpec((3, 2, Q_BLOCK), lambda h: (0, 0, 0)),
                  pl.BlockSpec(memory_space=pltpu.SMEM)],
        out_specs=pl.BlockSpec((1, T5_KINDS, Q_BLOCK, Q_BLOCK), lambda h: (h, 0, 0, 0)),
        out_shape=jax.ShapeDtypeStruct((T5_HEADS, T5_KINDS, Q_BLOCK, Q_BLOCK), F32),
        compiler_params=_cparams(("arbitrary",)),
        name="t5_blocks",
    )(idx, t5_table)


def _na_table_kernel(rpb_ref, o_ref):
    shape = (GRID_W, Q_BLOCK)
    c = lax.broadcasted_iota(jnp.int32, shape, 0)
    u = lax.broadcasted_iota(jnp.int32, shape, 1)
    k_hi = u >= GRID_W
    kc = jnp.bitwise_and(u, GRID_W - 1)
    cs = jnp.clip(c - NA_WIN_COLS // 2, 0, GRID_W - NA_WIN_COLS)
    col_ok = (kc >= cs) & (kc < cs + NA_WIN_COLS)

    def toeplitz(dr, lane_off):
        if abs(dr) > NA_WIN_ROWS - 1:
            return jnp.full(shape, NEG_INF, F32)
        row = dr + NA_WIN_ROWS - 1
        vec = jnp.broadcast_to(rpb_ref[0, row:row + 1, :], shape)
        shift = (lane_off - (NA_WIN_COLS - 1)) % Q_BLOCK
        return pltpu.roll(vec, shift, 1, stride=1, stride_axis=0)

    for d in range(NA_REL_BLOCKS):
        dblk = d - NA_REL_BLOCKS // 2
        halves = []
        for a in range(2):
            even = toeplitz(2 * dblk - a, 0)
            odd = toeplitz(2 * dblk + 1 - a, GRID_W)
            halves.append(jnp.where(col_ok, jnp.where(k_hi, odd, even), NEG_INF))
        o_ref[0, d] = jnp.concatenate(halves, axis=0) * LOG2E


def _na_table(rpb):
    heads, n_rows, n_cols = rpb.shape
    padded = jnp.pad(rpb, ((0, 0), (0, NA_RPB_ROWS_PADDED - n_rows), (0, Q_BLOCK - n_cols)))
    return pl.pallas_call(
        _na_table_kernel,
        grid=(heads,),
        in_specs=[pl.BlockSpec((1, NA_RPB_ROWS_PADDED, Q_BLOCK), lambda h: (h, 0, 0))],
        out_specs=pl.BlockSpec((1, NA_REL_BLOCKS, Q_BLOCK, Q_BLOCK), lambda h: (h, 0, 0, 0)),
        out_shape=jax.ShapeDtypeStruct((heads, NA_REL_BLOCKS, Q_BLOCK, Q_BLOCK), F32),
        compiler_params=_cparams(("arbitrary",)),
        name="na_table",
    )(padded)


def _rope(x, cos, sin_signed):
    lane = lax.broadcasted_iota(jnp.int32, x.shape, 1)
    even = jnp.bitwise_and(lane, 1) == 0
    swapped = jnp.where(even, pltpu.roll(x, HEAD_DIM - 1, 1), pltpu.roll(x, 1, 1))
    return x * cos + swapped * sin_signed


def _prep_kernel(bq_ref, bk_ref, cos_ref, sin_ref, qg_ref, kg_ref, qo_ref, ko_ref):
    cos = cos_ref[...]
    sin = sin_ref[...]
    for h in range(AX_HEADS):
        cols = slice(h * HEAD_DIM, (h + 1) * HEAD_DIM)
        qh = _rope(_rms(bq_ref[0, :, cols].astype(F32), qg_ref[...]), cos, sin)
        qo_ref[0, :, cols] = (qh * LOGIT_SCALE).astype(qo_ref.dtype)
    for h in range(AX_KV_HEADS):
        cols = slice(h * HEAD_DIM, (h + 1) * HEAD_DIM)
        kh = _rope(_rms(bk_ref[0, :, cols].astype(F32), kg_ref[...]), cos, sin)
        ko_ref[0, :, cols] = kh.astype(ko_ref.dtype)


def _prep(proj, cos, sin_signed, q_gain, k_gain, rows=512):
    b, s, _ = proj.shape
    kw = AX_KV_HEADS * HEAD_DIM
    return pl.pallas_call(
        _prep_kernel,
        grid=(b, s // rows),
        in_specs=[
            pl.BlockSpec((1, rows, GROUP_W), lambda bb, r: (bb, r, OFF_BQ // GROUP_W)),
            pl.BlockSpec((1, rows, kw), lambda bb, r: (bb, r, OFF_BK // kw)),
            pl.BlockSpec((rows, HEAD_DIM), lambda bb, r: (r, 0)),
            pl.BlockSpec((rows, HEAD_DIM), lambda bb, r: (r, 0)),
            pl.BlockSpec((1, HEAD_DIM), lambda bb, r: (0, 0)),
            pl.BlockSpec((1, HEAD_DIM), lambda bb, r: (0, 0)),
        ],
        out_specs=[pl.BlockSpec((1, rows, GROUP_W), lambda bb, r: (bb, r, 0)),
                   pl.BlockSpec((1, rows, kw), lambda bb, r: (bb, r, 0))],
        out_shape=[jax.ShapeDtypeStruct((b, s, GROUP_W), BF16),
                   jax.ShapeDtypeStruct((b, s, kw), BF16)],
        compiler_params=_cparams(("parallel", "parallel")),
        name="prep",
    )(proj, proj, cos, sin_signed, q_gain.reshape(1, -1), k_gain.reshape(1, -1))


def _na_row_penalty(i, a, key_block, grid_rows, key_is_odd):
    first = min(max(2 * i + a - NA_WIN_ROWS // 2, 0), grid_rows - NA_WIN_ROWS)
    ok = [first <= 2 * key_block + half < first + NA_WIN_ROWS for half in range(2)]
    if all(ok):
        return None
    if not any(ok):
        return jnp.full((1, Q_BLOCK), NEG_INF, F32)
    return jnp.where(key_is_odd == ok[1], 0.0, NEG_INF)


def _attn_a_kernel(q_ref, k_ref, v_ref, tb_ref, o_ref, *, n_blocks):
    grid_rows = 2 * n_blocks
    union = NA_GROUP + NA_SPAN_BLOCKS - 1
    key_is_odd = lax.broadcasted_iota(jnp.int32, (1, Q_BLOCK), 1) >= GRID_W
    for i0 in range(0, n_blocks, NA_GROUP):
        su = min(max(i0 - 2, 0), n_blocks - union)
        keys = slice(su * Q_BLOCK, (su + union) * Q_BLOCK)
        s_union = _dot_nt(q_ref[0, i0 * Q_BLOCK:(i0 + NA_GROUP) * Q_BLOCK, :], k_ref[0, keys, :])
        e_rows, sums = [], []
        for c in range(NA_GROUP):
            i = i0 + c
            sb = min(max(i - 2, 0), n_blocks - NA_SPAN_BLOCKS)
            off = sb - su
            pieces = []
            for j in range(NA_SPAN_BLOCKS):
                blk = tb_ref[0, sb + j - i + NA_REL_BLOCKS // 2]
                halves = []
                for a in range(2):
                    half = blk[a * GRID_W:(a + 1) * GRID_W]
                    pen = _na_row_penalty(i, a, sb + j, grid_rows, key_is_odd)
                    halves.append(half if pen is None else half + pen)
                pieces.append(jnp.concatenate(halves, axis=0))
            s = s_union[c * Q_BLOCK:(c + 1) * Q_BLOCK, off * Q_BLOCK:(off + NA_SPAN_BLOCKS) * Q_BLOCK]
            e, l = _softmax_parts(s + jnp.concatenate(pieces, axis=1))
            pad = [jnp.zeros((Q_BLOCK, n * Q_BLOCK), BF16)
                   for n in (off, union - NA_SPAN_BLOCKS - off)]
            e_rows.append(jnp.concatenate(
                [p for p in (pad[0], e.astype(BF16), pad[1]) if p.shape[1]], axis=1))
            sums.append(l)
        o = _dot(jnp.concatenate(e_rows, axis=0), v_ref[0, keys, :])
        for c in range(NA_GROUP):
            rows = slice((i0 + c) * Q_BLOCK, (i0 + c + 1) * Q_BLOCK)
            o_ref[0, rows, :] = (o[c * Q_BLOCK:(c + 1) * Q_BLOCK] * (1.0 / sums[c])).astype(o_ref.dtype)


def _attn_a(proj, table):
    b, s, _ = proj.shape
    nb = s // Q_BLOCK
    hd = HEAD_DIM
    return pl.pallas_call(
        functools.partial(_attn_a_kernel, n_blocks=nb),
        grid=(NA_HEADS, b),
        in_specs=[
            pl.BlockSpec((1, s, hd), lambda h, bb: (bb, 0, OFF_AQ // hd + h)),
            pl.BlockSpec((1, s, hd), lambda h, bb: (bb, 0, OFF_AK // hd + h)),
            pl.BlockSpec((1, s, hd), lambda h, bb: (bb, 0, OFF_AV // hd + h)),
            pl.BlockSpec((1, NA_REL_BLOCKS, Q_BLOCK, Q_BLOCK), lambda h, bb: (h, 0, 0, 0)),
        ],
        out_specs=pl.BlockSpec((1, s, hd), lambda h, bb: (bb, 0, MIX_A // hd + h)),
        out_shape=jax.ShapeDtypeStruct((b, s, D_MODEL), BF16),
        compiler_params=_cparams(("parallel", "parallel")),
        name="attn_a",
    )(proj, proj, proj, table)


def _attn_b_kernel(q_ref, k_ref, v_ref, mix_ref, o_ref):
    del mix_ref
    k = k_ref[0]
    v = v_ref[0]
    for g in range(AX_HEADS // AX_KV_HEADS):
        cols = slice(g * HEAD_DIM, (g + 1) * HEAD_DIM)
        e, l = _softmax_parts(_dot_nt(q_ref[0, :, cols], k))
        o = _dot(e.astype(BF16), v) * (1.0 / l)
        o_ref[0, :, cols] = o.astype(o_ref.dtype)


def _attn_b(qn, kn, proj, mix, tq=1024):
    b, s, _ = proj.shape
    qw = (AX_HEADS // AX_KV_HEADS) * HEAD_DIM
    return pl.pallas_call(
        _attn_b_kernel,
        grid=(b, AX_KV_HEADS, s // tq),
        in_specs=[
            pl.BlockSpec((1, tq, qw), lambda bb, kv, i: (bb, i, kv)),
            pl.BlockSpec((1, s, HEAD_DIM), lambda bb, kv, i: (bb, 0, kv)),
            pl.BlockSpec((1, s, HEAD_DIM), lambda bb, kv, i: (bb, 0, OFF_BV // HEAD_DIM + kv)),
            pl.BlockSpec(memory_space=pl.ANY),
        ],
        out_specs=pl.BlockSpec((1, tq, qw), lambda bb, kv, i: (bb, i, MIX_B // qw + kv)),
        out_shape=jax.ShapeDtypeStruct(mix.shape, mix.dtype),
        input_output_aliases={3: 0},
        compiler_params=_cparams(("parallel", "parallel", "arbitrary")),
        name="attn_b",
    )(qn, kn, proj, mix)


def _attn_c_kernel(q_ref, k_ref, v_ref, t5_ref, sink_ref, mix_ref, o_ref, *, n_blocks, q_blocks):
    del mix_ref
    kv = pl.program_id(1)
    groups = SW_HEADS // SW_KV_HEADS
    sink = jnp.concatenate(
        [jnp.full((Q_BLOCK, 1), sink_ref[kv * groups + g] * LOG2E, F32) for g in range(groups)],
        axis=0)
    for c in range(q_blocks):
        i = pl.program_id(2) * q_blocks + c
        rows = slice(c * Q_BLOCK, (c + 1) * Q_BLOCK)
        blocks = (jnp.maximum(i - 1, 0), i, jnp.minimum(i + 1, n_blocks - 1))
        kinds = (jnp.where(i > 0, T5_BAND0, T5_MASKED), T5_BAND0 + 1,
                 jnp.where(i < n_blocks - 1, T5_BAND0 + 2, T5_MASKED))
        starts = [pl.multiple_of(blk * Q_BLOCK, Q_BLOCK) for blk in blocks]
        q = jnp.concatenate(
            [q_ref[0, rows, g * HEAD_DIM:(g + 1) * HEAD_DIM] for g in range(groups)], axis=0)
        s = jnp.concatenate(
            [_dot_nt(q, k_ref[0, pl.ds(start, Q_BLOCK), :])
             + jnp.concatenate([t5_ref[g, kind] for g in range(groups)], axis=0)
             for start, kind in zip(starts, kinds)], axis=1)
        e, l = _softmax_parts(s, extra=sink)
        v = jnp.concatenate([v_ref[0, pl.ds(start, Q_BLOCK), :] for start in starts], axis=0)
        o = _dot(e.astype(BF16), v) * (1.0 / l)
        for g in range(groups):
            o_ref[0, rows, g * HEAD_DIM:(g + 1) * HEAD_DIM] = (
                o[g * Q_BLOCK:(g + 1) * Q_BLOCK].astype(o_ref.dtype))


def _attn_c(proj, t5_blocks, sink, mix, q_blocks=16):
    b, s, _ = proj.shape
    nb = s // Q_BLOCK
    groups = SW_HEADS // SW_KV_HEADS
    qw = groups * HEAD_DIM
    hd = HEAD_DIM
    tq = q_blocks * Q_BLOCK
    return pl.pallas_call(
        functools.partial(_attn_c_kernel, n_blocks=nb, q_blocks=q_blocks),
        grid=(b, SW_KV_HEADS, nb // q_blocks),
        in_specs=[
            pl.BlockSpec((1, tq, qw), lambda bb, kv, i: (bb, i, OFF_CQ // qw + kv)),
            pl.BlockSpec((1, s, hd), lambda bb, kv, i: (bb, 0, OFF_CK // hd + kv)),
            pl.BlockSpec((1, s, hd), lambda bb, kv, i: (bb, 0, OFF_CV // hd + kv)),
            pl.BlockSpec((groups, T5_KINDS, Q_BLOCK, Q_BLOCK), lambda bb, kv, i: (kv, 0, 0, 0)),
            pl.BlockSpec(memory_space=pltpu.SMEM),
            pl.BlockSpec(memory_space=pl.ANY),
        ],
        out_specs=pl.BlockSpec((1, tq, qw), lambda bb, kv, i: (bb, i, MIX_C // qw + kv)),
        out_shape=jax.ShapeDtypeStruct(mix.shape, mix.dtype),
        input_output_aliases={5: 0},
        compiler_params=_cparams(("parallel", "parallel", "arbitrary")),
        name="attn_c",
    )(proj, proj, proj, t5_blocks, sink, mix)


def _attn_d_kernel(q1_ref, q2_ref, k1_ref, k2_ref, v_ref, t5_ref, lam_ref, g_ref, mix_ref, o_ref,
                   *, n_blocks, q_blocks, lambda_init):
    del mix_ref
    lp = lam_ref[...]
    lam = (jnp.exp(jnp.sum(lp[0:1] * lp[1:2], axis=-1, keepdims=True))
           - jnp.exp(jnp.sum(lp[2:3] * lp[3:4], axis=-1, keepdims=True)) + lambda_init)
    k1 = k1_ref[0]
    k2 = k2_ref[0]
    v = v_ref[0]
    for c in range(q_blocks):
        i = pl.program_id(2) * q_blocks + c
        rows = slice(c * Q_BLOCK, (c + 1) * Q_BLOCK)
        bias = jnp.concatenate(
            [t5_ref[0, jnp.clip(kj - i + T5_BAND0 + 1, T5_FAR_LO, T5_FAR_HI)]
             for kj in range(n_blocks)], axis=1)
        e1, l1 = _softmax_parts(_dot_nt(q1_ref[0, rows, :], k1) + bias)
        e2, l2 = _softmax_parts(_dot_nt(q2_ref[0, rows, :], k2) + bias)
        ev = _dot(jnp.concatenate([e1.astype(BF16), e2.astype(BF16)], axis=0), v)
        od = ev[:Q_BLOCK] * (1.0 / l1) - ev[Q_BLOCK:] * (lam / l2)
        o_ref[0, rows, :] = (_rms(od, g_ref[...]) * (1.0 - lambda_init)).astype(o_ref.dtype)


def _attn_d(proj, t5_blocks, lam_params, subln, lambda_init, mix, q_blocks=16):
    b, s, _ = proj.shape
    nb = s // Q_BLOCK
    hd = HEAD_DIM
    tq = q_blocks * Q_BLOCK
    return pl.pallas_call(
        functools.partial(_attn_d_kernel, n_blocks=nb, q_blocks=q_blocks, lambda_init=lambda_init),
        grid=(b, DF_HEADS, nb // q_blocks),
        in_specs=[
            pl.BlockSpec((1, tq, hd), lambda bb, h, i: (bb, i, OFF_DQ // hd + h)),
            pl.BlockSpec((1, tq, hd), lambda bb, h, i: (bb, i, OFF_DQ // hd + DF_HEADS + h)),
            pl.BlockSpec((1, s, hd), lambda bb, h, i: (bb, 0, OFF_DK // hd + h)),
            pl.BlockSpec((1, s, hd), lambda bb, h, i: (bb, 0, OFF_DK // hd + DF_HEADS + h)),
            pl.BlockSpec((1, s, DF_V_DIM), lambda bb, h, i: (bb, 0, OFF_DV // DF_V_DIM + h)),
            pl.BlockSpec((1, T5_KINDS, Q_BLOCK, Q_BLOCK), lambda bb, h, i: (SW_HEADS + h, 0, 0, 0)),
            pl.BlockSpec((4, hd), lambda bb, h, i: (0, 0)),
            pl.BlockSpec((1, DF_V_DIM), lambda bb, h, i: (0, 0)),
            pl.BlockSpec(memory_space=pl.ANY),
        ],
        out_specs=pl.BlockSpec((1, tq, DF_V_DIM), lambda bb, h, i: (bb, i, MIX_D // DF_V_DIM + h)),
        out_shape=jax.ShapeDtypeStruct(mix.shape, mix.dtype),
        input_output_aliases={8: 0},
        compiler_params=_cparams(("parallel", "parallel", "arbitrary")),
        name="attn_d",
    )(proj, proj, proj, proj, proj, t5_blocks, lam_params, subln.reshape(1, -1), mix)


def _rope_tables(s):
    pos = jnp.arange(s, dtype=jnp.int32)
    row = (pos // GRID_W).astype(F32)
    col = (pos % GRID_W).astype(F32)
    n_pairs = HEAD_DIM // 4
    inv = ROPE_THETA ** (-jnp.arange(n_pairs, dtype=F32) / n_pairs)
    ang = jnp.concatenate([row[:, None] * inv, col[:, None] * inv], axis=-1)
    cos = jnp.repeat(jnp.cos(ang), 2, axis=-1)
    sin = jnp.repeat(jnp.sin(ang), 2, axis=-1)
    sign = jnp.tile(jnp.asarray([-1.0, 1.0], F32), HEAD_DIM // 2)
    return cos, sin * sign


def _q_col_scale():
    col = jnp.arange(D_IN, dtype=jnp.int32)
    is_q = (((col >= OFF_AQ) & (col < OFF_AK)) | ((col >= OFF_CQ) & (col < OFF_CK))
            | ((col >= OFF_DQ) & (col < OFF_DK)))
    return jnp.where(is_q, LOGIT_SCALE, 1.0).astype(F32).reshape(1, D_IN)


def kernel(x, ln_attn_pre, ln_attn_post, ln_mlp_pre, ln_mlp_post, w_in, w_out, na_rpb,
           ax_q_norm, ax_k_norm, sw_sink, df_lambda, df_subln, t5_table, w_mlp_in, w_mlp_out):
    b, s, d = x.shape
    depth = w_in.shape[0]
    t = b * s
    cos, sin_signed = _rope_tables(s)
    col_scale = _q_col_scale()
    t5_blocks = _t5_blocks(t5_table)
    xf = x.reshape(t, d)
    h = _norm_cast(xf, ln_attn_pre[0])
    for l in range(depth):
        proj = _matmul(h, w_in, l, BF16, col_scale=col_scale, name="in_proj").reshape(b, s, D_IN)
        qn_b, kn_b = _prep(proj, cos, sin_signed, ax_q_norm[l], ax_k_norm[l])
        mix = _attn_a(proj, _na_table(na_rpb[l]))
        mix = _attn_b(qn_b, kn_b, proj, mix)
        mix = _attn_c(proj, t5_blocks, sw_sink[l], mix)
        lambda_init = 0.8 - 0.6 * math.exp(-0.3 * l)
        mix = _attn_d(proj, t5_blocks, df_lambda[l], df_subln[l], lambda_init, mix)
        y = _matmul(mix.reshape(t, d), w_out, l, BF16, name="out_proj")
        xf, h = _resid_norm(xf, y, ln_attn_post[l], ln_mlp_pre[l])
        u = _matmul(h, w_mlp_in, l, BF16, act=_relu2, name="mlp_in")
        y = _matmul_ksplit(u, w_mlp_out, l, BF16, name="mlp_out")
        g_next = ln_attn_pre[l + 1] if l + 1 < depth else None
        xf, h = _resid_norm(xf, y, ln_mlp_post[l], g_next)
    return xf.reshape(b, s, d)
```

```python
import functools
import math

import jax
import jax.numpy as jnp
from jax import lax
from jax.experimental import pallas as pl
from jax.experimental.pallas import tpu as pltpu

F32 = jnp.float32
BF16 = jnp.bfloat16

D_MODEL = 4096
GRID_W = 64
HEAD_DIM = 128
Q_BLOCK = 128
GROUP_W = D_MODEL // 4
NA_HEADS = GROUP_W // HEAD_DIM
NA_WIN_ROWS = 8
NA_WIN_COLS = 16
AX_HEADS = GROUP_W // HEAD_DIM
AX_KV_HEADS = AX_HEADS // 4
ROPE_THETA = 10000.0
SW_HEADS = GROUP_W // HEAD_DIM
SW_KV_HEADS = SW_HEADS // 4
SW_WINDOW = 128
DF_V_DIM = 2 * HEAD_DIM
DF_HEADS = GROUP_W // DF_V_DIM
T5_BUCKETS = 32
T5_MAX_DIST = 128
T5_HEADS = SW_HEADS + DF_HEADS
EPS = 1e-6
SCALE = HEAD_DIM ** -0.5
LOG2E = math.log2(math.e)
LOGIT_SCALE = SCALE * LOG2E

_IN_WIDTHS = (
    GROUP_W, GROUP_W, GROUP_W,
    GROUP_W, AX_KV_HEADS * HEAD_DIM, AX_KV_HEADS * HEAD_DIM,
    GROUP_W, SW_KV_HEADS * HEAD_DIM, SW_KV_HEADS * HEAD_DIM,
    2 * DF_HEADS * HEAD_DIM, 2 * DF_HEADS * HEAD_DIM, DF_HEADS * DF_V_DIM,
)
(OFF_AQ, OFF_AK, OFF_AV, OFF_BQ, OFF_BK, OFF_BV,
 OFF_CQ, OFF_CK, OFF_CV, OFF_DQ, OFF_DK, OFF_DV) = (
    sum(_IN_WIDTHS[:n]) for n in range(len(_IN_WIDTHS)))
D_IN = sum(_IN_WIDTHS)
MIX_A, MIX_B, MIX_C, MIX_D = 0, GROUP_W, 2 * GROUP_W, 3 * GROUP_W

NA_SPAN_BLOCKS = 5
NA_REL_BLOCKS = 9
NA_GROUP = 4
DF_GROUP = 4
AX_CHAIN_ROWS = 256
NA_RPB_ROWS = 2 * NA_WIN_ROWS - 1
NA_RPB_COLS = 2 * NA_WIN_COLS - 1
NA_RPB_ROWS_PADDED = 16

T5_FAR_LO, T5_BAND0, T5_FAR_HI, T5_MASKED, T5_KINDS = 0, 1, 4, 5, 6

V7X_VMEM_BYTES = 64 * 1024 * 1024
VMEM_LIMIT = V7X_VMEM_BYTES - 4 * 1024 * 1024
NEG_INF = float("-inf")


def _cparams(sem):
    return pltpu.CompilerParams(dimension_semantics=sem, vmem_limit_bytes=VMEM_LIMIT)


def _dot(a, b):
    return jnp.dot(a, b, preferred_element_type=F32)


def _dot_nt(a, b):
    return lax.dot_general(a, b, (((1,), (1,)), ((), ())), preferred_element_type=F32)


def _rms(x, g):
    return x * lax.rsqrt(jnp.mean(x * x, axis=-1, keepdims=True) + EPS) * g


def _softmax_parts(s, extra=None):
    m = jnp.max(s, axis=-1, keepdims=True)
    if extra is not None:
        m = jnp.maximum(m, extra)
    e = jnp.exp2(s - m)
    l = jnp.sum(e, axis=-1, keepdims=True)
    if extra is not None:
        l = l + jnp.exp2(extra - m)
    return e, l


def _norm_cast_kernel(x_ref, g_ref, h_ref):
    h_ref[...] = _rms(x_ref[...], g_ref[...]).astype(h_ref.dtype)


def _norm_cast(x, g, rows=512):
    t, d = x.shape
    return pl.pallas_call(
        _norm_cast_kernel,
        grid=(t // rows,),
        in_specs=[pl.BlockSpec((rows, d), lambda i: (i, 0)),
                  pl.BlockSpec((1, d), lambda i: (0, 0))],
        out_specs=pl.BlockSpec((rows, d), lambda i: (i, 0)),
        out_shape=jax.ShapeDtypeStruct((t, d), BF16),
        compiler_params=_cparams(("parallel",)),
        name="norm_cast",
    )(x, g.reshape(1, d))


def _resid_norm_kernel(x_ref, y_ref, gp_ref, gn_ref, xo_ref, h_ref):
    xn = x_ref[...] + _rms(y_ref[...].astype(F32), gp_ref[...])
    xo_ref[...] = xn
    h_ref[...] = _rms(xn, gn_ref[...]).astype(h_ref.dtype)


def _resid_kernel(x_ref, y_ref, gp_ref, xo_ref):
    xo_ref[...] = x_ref[...] + _rms(y_ref[...].astype(F32), gp_ref[...])


def _resid_norm(x, y, g_post, g_next, rows=512):
    t, d = x.shape
    row_spec = pl.BlockSpec((rows, d), lambda i: (i, 0))
    g_spec = pl.BlockSpec((1, d), lambda i: (0, 0))
    if g_next is None:
        return pl.pallas_call(
            _resid_kernel, grid=(t // rows,),
            in_specs=[row_spec, row_spec, g_spec], out_specs=row_spec,
            out_shape=jax.ShapeDtypeStruct((t, d), F32),
            compiler_params=_cparams(("parallel",)), name="resid",
        )(x, y, g_post.reshape(1, d)), None
    return pl.pallas_call(
        _resid_norm_kernel, grid=(t // rows,),
        in_specs=[row_spec, row_spec, g_spec, g_spec], out_specs=[row_spec, row_spec],
        out_shape=[jax.ShapeDtypeStruct((t, d), F32), jax.ShapeDtypeStruct((t, d), BF16)],
        compiler_params=_cparams(("parallel",)), name="resid_norm",
    )(x, y, g_post.reshape(1, d), g_next.reshape(1, d))


def _relu2(r):
    r = jnp.maximum(r, 0.0)
    return r * r


def _mm_kernel(a_ref, w_ref, *rest, act, col_scaled):
    o_ref = rest[-1]
    r = _dot(a_ref[...], w_ref[...].astype(BF16))
    if act is not None:
        r = act(r)
    if col_scaled:
        r = r * rest[0][...]
    o_ref[...] = r.astype(o_ref.dtype)


def _matmul(a, w, layer, out_dtype, *, tm=2048, tn=512, act=None, col_scale=None, name="matmul"):
    m, kdim = a.shape
    _, _, n = w.shape
    in_specs = [pl.BlockSpec((tm, kdim), lambda i, j: (i, 0)),
                pl.BlockSpec((None, kdim, tn), lambda i, j: (layer, 0, j))]
    args = [a, w]
    if col_scale is not None:
        in_specs.append(pl.BlockSpec((1, tn), lambda i, j: (0, j)))
        args.append(col_scale)
    return pl.pallas_call(
        functools.partial(_mm_kernel, act=act, col_scaled=col_scale is not None),
        grid=(m // tm, n // tn),
        in_specs=in_specs,
        out_specs=pl.BlockSpec((tm, tn), lambda i, j: (i, j)),
        out_shape=jax.ShapeDtypeStruct((m, n), out_dtype),
        compiler_params=_cparams(("parallel", "parallel")),
        name=name,
    )(*args)


def _mm_ksplit_kernel(a_ref, w_ref, o_ref, acc_ref):
    k = pl.program_id(2)

    @pl.when(k == 0)
    def _():
        acc_ref[...] = jnp.zeros(acc_ref.shape, acc_ref.dtype)

    acc_ref[...] += _dot(a_ref[...], w_ref[...].astype(BF16))

    @pl.when(k == pl.num_programs(2) - 1)
    def _():
        o_ref[...] = acc_ref[...].astype(o_ref.dtype)


def _matmul_ksplit(a, w, layer, out_dtype, *, tm=2048, tn=1024, tk=2048, name="matmul_ksplit"):
    m, kdim = a.shape
    _, _, n = w.shape
    return pl.pallas_call(
        _mm_ksplit_kernel,
        grid=(m // tm, n // tn, kdim // tk),
        in_specs=[pl.BlockSpec((tm, tk), lambda i, j, k: (i, k)),
                  pl.BlockSpec((None, tk, tn), lambda i, j, k: (layer, k, j))],
        out_specs=pl.BlockSpec((tm, tn), lambda i, j, k: (i, j)),
        out_shape=jax.ShapeDtypeStruct((m, n), out_dtype),
        scratch_shapes=[pltpu.VMEM((tm, tn), F32)],
        compiler_params=_cparams(("parallel", "parallel", "arbitrary")),
        name=name,
    )(a, w)


def _t5_bucket(rel):
    nb = T5_BUCKETS // 2
    max_exact = nb // 2
    base = jnp.where(rel > 0, nb, 0)
    n = jnp.abs(rel)
    n_f = jnp.maximum(n, 1).astype(F32)
    large = max_exact + (jnp.log(n_f / max_exact) / math.log(T5_MAX_DIST / max_exact)
                         * (nb - max_exact)).astype(jnp.int32)
    large = jnp.minimum(large, nb - 1)
    return base + jnp.where(n < max_exact, n, large)


def _t5_blocks_kernel(idx_ref, tab_ref, o_ref):
    h = pl.program_id(0)
    shape = (Q_BLOCK, Q_BLOCK)
    t = lax.broadcasted_iota(jnp.int32, shape, 0)
    u = lax.broadcasted_iota(jnp.int32, shape, 1)
    o_ref[0, T5_FAR_LO] = jnp.full(shape, tab_ref[T5_BUCKETS // 2 - 1, h] * LOG2E, F32)
    o_ref[0, T5_FAR_HI] = jnp.full(shape, tab_ref[T5_BUCKETS - 1, h] * LOG2E, F32)
    o_ref[0, T5_MASKED] = jnp.full(shape, NEG_INF, F32)
    for j in range(3):
        idx = idx_ref[j]
        diag = jnp.zeros(idx.shape, F32)
        for b in range(T5_BUCKETS):
            diag = jnp.where(idx == b, tab_ref[b, h], diag)
        upper = pltpu.roll(jnp.broadcast_to(diag[0:1], shape), 0, 1, stride=1, stride_axis=0)
        lower = pltpu.roll(jnp.broadcast_to(diag[1:2], shape), 0, 1, stride=1, stride_axis=0)
        band = jnp.where(u >= t, upper, lower)
        rel = u + (j - 1) * Q_BLOCK - t
        allowed = (jnp.abs(rel) <= SW_WINDOW) | (h >= SW_HEADS)
        o_ref[0, T5_BAND0 + j] = jnp.where(allowed, band, NEG_INF) * LOG2E


def _t5_blocks(t5_table):
    k = jnp.arange(Q_BLOCK, dtype=jnp.int32)
    block_off = (jnp.arange(3, dtype=jnp.int32) - 1) * Q_BLOCK
    rel = block_off[:, None, None] + jnp.stack([k, k - Q_BLOCK])[None]
    idx = _t5_bucket(rel).astype(jnp.int32)
    return pl.pallas_call(
        _t5_blocks_kernel,
        grid=(T5_HEADS,),
        in_specs=[pl.BlockSpec((3, 2, Q_BLOCK), lambda h: (0, 0, 0)),
                  pl.BlockSpec(memory_space=pltpu.SMEM)],
        out_specs=pl.BlockSpec((1, T5_KINDS, Q_BLOCK, Q_BLOCK), lambda h: (h, 0, 0, 0)),
        out_shape=jax.ShapeDtypeStruct((T5_HEADS, T5_KINDS, Q_BLOCK, Q_BLOCK), F32),
        compiler_params=_cparams(("arbitrary",)),
        name="t5_blocks",
    )(idx, t5_table)


def _na_table_kernel(rpb_ref, o_ref):
    shape = (GRID_W, Q_BLOCK)
    c = lax.broadcasted_iota(jnp.int32, shape, 0)
    u = lax.broadcasted_iota(jnp.int32, shape, 1)
    k_hi = u >= GRID_W
    kc = jnp.bitwise_and(u, GRID_W - 1)
    cs = jnp.clip(c - NA_WIN_COLS // 2, 0, GRID_W - NA_WIN_COLS)
    col_ok = (kc >= cs) & (kc < cs + NA_WIN_COLS)

    def toeplitz(dr, lane_off):
        if abs(dr) > NA_WIN_ROWS - 1:
            return jnp.full(shape, NEG_INF, F32)
        row = dr + NA_WIN_ROWS - 1
        vec = jnp.broadcast_to(rpb_ref[0, row:row + 1, :], shape)
        shift = (lane_off - (NA_WIN_COLS - 1)) % Q_BLOCK
        return pltpu.roll(vec, shift, 1, stride=1, stride_axis=0)

    for d in range(NA_REL_BLOCKS):
        dblk = d - NA_REL_BLOCKS // 2
        halves = []
        for a in range(2):
            even = toeplitz(2 * dblk - a, 0)
            odd = toeplitz(2 * dblk + 1 - a, GRID_W)
            halves.append(jnp.where(col_ok, jnp.where(k_hi, odd, even), NEG_INF))
        o_ref[0, d] = jnp.concatenate(halves, axis=0) * LOG2E


def _na_table(rpb):
    heads, n_rows, n_cols = rpb.shape
    padded = jnp.pad(rpb, ((0, 0), (0, NA_RPB_ROWS_PADDED - n_rows), (0, Q_BLOCK - n_cols)))
    return pl.pallas_call(
        _na_table_kernel,
        grid=(heads,),
        in_specs=[pl.BlockSpec((1, NA_RPB_ROWS_PADDED, Q_BLOCK), lambda h: (h, 0, 0))],
        out_specs=pl.BlockSpec((1, NA_REL_BLOCKS, Q_BLOCK, Q_BLOCK), lambda h: (h, 0, 0, 0)),
        out_shape=jax.ShapeDtypeStruct((heads, NA_REL_BLOCKS, Q_BLOCK, Q_BLOCK), F32),
        compiler_params=_cparams(("arbitrary",)),
        name="na_table",
    )(padded)


def _rope(x, cos, sin_signed):
    lane = lax.broadcasted_iota(jnp.int32, x.shape, 1)
    even = jnp.bitwise_and(lane, 1) == 0
    swapped = jnp.where(even, pltpu.roll(x, HEAD_DIM - 1, 1), pltpu.roll(x, 1, 1))
    return x * cos + swapped * sin_signed


def _prep_kernel(bq_ref, bk_ref, cos_ref, sin_ref, qg_ref, kg_ref, qo_ref, ko_ref):
    cos = cos_ref[...]
    sin = sin_ref[...]
    for h in range(AX_HEADS):
        cols = slice(h * HEAD_DIM, (h + 1) * HEAD_DIM)
        qh = _rope(_rms(bq_ref[0, :, cols].astype(F32), qg_ref[...]), cos, sin)
        qo_ref[0, :, cols] = (qh * LOGIT_SCALE).astype(qo_ref.dtype)
    for h in range(AX_KV_HEADS):
        cols = slice(h * HEAD_DIM, (h + 1) * HEAD_DIM)
        kh = _rope(_rms(bk_ref[0, :, cols].astype(F32), kg_ref[...]), cos, sin)
        ko_ref[0, :, cols] = kh.astype(ko_ref.dtype)


def _prep(proj, cos, sin_signed, q_gain, k_gain, rows=512):
    b, s, _ = proj.shape
    kw = AX_KV_HEADS * HEAD_DIM
    return pl.pallas_call(
        _prep_kernel,
        grid=(b, s // rows),
        in_specs=[
            pl.BlockSpec((1, rows, GROUP_W), lambda bb, r: (bb, r, OFF_BQ // GROUP_W)),
            pl.BlockSpec((1, rows, kw), lambda bb, r: (bb, r, OFF_BK // kw)),
            pl.BlockSpec((rows, HEAD_DIM), lambda bb, r: (r, 0)),
            pl.BlockSpec((rows, HEAD_DIM), lambda bb, r: (r, 0)),
            pl.BlockSpec((1, HEAD_DIM), lambda bb, r: (0, 0)),
            pl.BlockSpec((1, HEAD_DIM), lambda bb, r: (0, 0)),
        ],
        out_specs=[pl.BlockSpec((1, rows, GROUP_W), lambda bb, r: (bb, r, 0)),
                   pl.BlockSpec((1, rows, kw), lambda bb, r: (bb, r, 0))],
        out_shape=[jax.ShapeDtypeStruct((b, s, GROUP_W), BF16),
                   jax.ShapeDtypeStruct((b, s, kw), BF16)],
        compiler_params=_cparams(("parallel", "parallel")),
        name="prep",
    )(proj, proj, cos, sin_signed, q_gain.reshape(1, -1), k_gain.reshape(1, -1))


def _na_row_penalty(i, a, key_block, grid_rows, key_is_odd):
    first = min(max(2 * i + a - NA_WIN_ROWS // 2, 0), grid_rows - NA_WIN_ROWS)
    ok = [first <= 2 * key_block + half < first + NA_WIN_ROWS for half in range(2)]
    if all(ok):
        return None
    if not any(ok):
        return jnp.full((1, Q_BLOCK), NEG_INF, F32)
    return jnp.where(key_is_odd == ok[1], 0.0, NEG_INF)


def _attn_a_kernel(q_ref, k_ref, v_ref, tb_ref, o_ref, *, n_blocks):
    grid_rows = 2 * n_blocks
    union = NA_GROUP + NA_SPAN_BLOCKS - 1
    key_is_odd = lax.broadcasted_iota(jnp.int32, (1, Q_BLOCK), 1) >= GRID_W
    for i0 in range(0, n_blocks, NA_GROUP):
        su = min(max(i0 - 2, 0), n_blocks - union)
        keys = slice(su * Q_BLOCK, (su + union) * Q_BLOCK)
        s_union = _dot_nt(q_ref[0, i0 * Q_BLOCK:(i0 + NA_GROUP) * Q_BLOCK, :], k_ref[0, keys, :])
        e_rows, sums = [], []
        for c in range(NA_GROUP):
            i = i0 + c
            sb = min(max(i - 2, 0), n_blocks - NA_SPAN_BLOCKS)
            off = sb - su
            pieces = []
            for j in range(NA_SPAN_BLOCKS):
                blk = tb_ref[0, sb + j - i + NA_REL_BLOCKS // 2]
                halves = []
                for a in range(2):
                    half = blk[a * GRID_W:(a + 1) * GRID_W]
                    pen = _na_row_penalty(i, a, sb + j, grid_rows, key_is_odd)
                    halves.append(half if pen is None else half + pen)
                pieces.append(jnp.concatenate(halves, axis=0))
            s = s_union[c * Q_BLOCK:(c + 1) * Q_BLOCK, off * Q_BLOCK:(off + NA_SPAN_BLOCKS) * Q_BLOCK]
            e, l = _softmax_parts(s + jnp.concatenate(pieces, axis=1))
            pad = [jnp.zeros((Q_BLOCK, n * Q_BLOCK), BF16)
                   for n in (off, union - NA_SPAN_BLOCKS - off)]
            e_rows.append(jnp.concatenate(
                [p for p in (pad[0], e.astype(BF16), pad[1]) if p.shape[1]], axis=1))
            sums.append(l)
        o = _dot(jnp.concatenate(e_rows, axis=0), v_ref[0, keys, :])
        for c in range(NA_GROUP):
            rows = slice((i0 + c) * Q_BLOCK, (i0 + c + 1) * Q_BLOCK)
            o_ref[0, rows, :] = (o[c * Q_BLOCK:(c + 1) * Q_BLOCK] * (1.0 / sums[c])).astype(o_ref.dtype)


def _attn_a(proj, table):
    b, s, _ = proj.shape
    nb = s // Q_BLOCK
    hd = HEAD_DIM
    return pl.pallas_call(
        functools.partial(_attn_a_kernel, n_blocks=nb),
        grid=(NA_HEADS, b),
        in_specs=[
            pl.BlockSpec((1, s, hd), lambda h, bb: (bb, 0, OFF_AQ // hd + h)),
            pl.BlockSpec((1, s, hd), lambda h, bb: (bb, 0, OFF_AK // hd + h)),
            pl.BlockSpec((1, s, hd), lambda h, bb: (bb, 0, OFF_AV // hd + h)),
            pl.BlockSpec((1, NA_REL_BLOCKS, Q_BLOCK, Q_BLOCK), lambda h, bb: (h, 0, 0, 0)),
        ],
        out_specs=pl.BlockSpec((1, s, hd), lambda h, bb: (bb, 0, MIX_A // hd + h)),
        out_shape=jax.ShapeDtypeStruct((b, s, D_MODEL), BF16),
        compiler_params=_cparams(("parallel", "parallel")),
        name="attn_a",
    )(proj, proj, proj, table)


def _attn_b_kernel(q_ref, k_ref, v_ref, mix_ref, o_ref):
    del mix_ref
    k = k_ref[0]
    v = v_ref[0]
    for r in range(q_ref.shape[1] // AX_CHAIN_ROWS):
        rows = slice(r * AX_CHAIN_ROWS, (r + 1) * AX_CHAIN_ROWS)
        for g in range(AX_HEADS // AX_KV_HEADS):
            cols = slice(g * HEAD_DIM, (g + 1) * HEAD_DIM)
            e, l = _softmax_parts(_dot_nt(q_ref[0, rows, cols], k))
            o = _dot(e.astype(BF16), v) * (1.0 / l)
            o_ref[0, rows, cols] = o.astype(o_ref.dtype)


def _attn_b(qn, kn, proj, mix, tq=1024):
    b, s, _ = proj.shape
    qw = (AX_HEADS // AX_KV_HEADS) * HEAD_DIM
    return pl.pallas_call(
        _attn_b_kernel,
        grid=(b, AX_KV_HEADS, s // tq),
        in_specs=[
            pl.BlockSpec((1, tq, qw), lambda bb, kv, i: (bb, i, kv)),
            pl.BlockSpec((1, s, HEAD_DIM), lambda bb, kv, i: (bb, 0, kv)),
            pl.BlockSpec((1, s, HEAD_DIM), lambda bb, kv, i: (bb, 0, OFF_BV // HEAD_DIM + kv)),
            pl.BlockSpec(memory_space=pl.ANY),
        ],
        out_specs=pl.BlockSpec((1, tq, qw), lambda bb, kv, i: (bb, i, MIX_B // qw + kv)),
        out_shape=jax.ShapeDtypeStruct(mix.shape, mix.dtype),
        input_output_aliases={3: 0},
        compiler_params=_cparams(("parallel", "parallel", "arbitrary")),
        name="attn_b",
    )(qn, kn, proj, mix)


def _attn_c_kernel(q_ref, k_ref, v_ref, t5_ref, sink_ref, mix_ref, o_ref, *, n_blocks, q_blocks):
    del mix_ref
    kv = pl.program_id(1)
    groups = SW_HEADS // SW_KV_HEADS
    sink = jnp.concatenate(
        [jnp.full((Q_BLOCK, 1), sink_ref[kv * groups + g] * LOG2E, F32) for g in range(groups)],
        axis=0)
    for c in range(q_blocks):
        i = pl.program_id(2) * q_blocks + c
        rows = slice(c * Q_BLOCK, (c + 1) * Q_BLOCK)
        blocks = (jnp.maximum(i - 1, 0), i, jnp.minimum(i + 1, n_blocks - 1))
        kinds = (jnp.where(i > 0, T5_BAND0, T5_MASKED), T5_BAND0 + 1,
                 jnp.where(i < n_blocks - 1, T5_BAND0 + 2, T5_MASKED))
        starts = [pl.multiple_of(blk * Q_BLOCK, Q_BLOCK) for blk in blocks]
        q = jnp.concatenate(
            [q_ref[0, rows, g * HEAD_DIM:(g + 1) * HEAD_DIM] for g in range(groups)], axis=0)
        s = jnp.concatenate(
            [_dot_nt(q, k_ref[0, pl.ds(start, Q_BLOCK), :])
             + jnp.concatenate([t5_ref[g, kind] for g in range(groups)], axis=0)
             for start, kind in zip(starts, kinds)], axis=1)
        e, l = _softmax_parts(s, extra=sink)
        v = jnp.concatenate([v_ref[0, pl.ds(start, Q_BLOCK), :] for start in starts], axis=0)
        o = _dot(e.astype(BF16), v) * (1.0 / l)
        for g in range(groups):
            o_ref[0, rows, g * HEAD_DIM:(g + 1) * HEAD_DIM] = (
                o[g * Q_BLOCK:(g + 1) * Q_BLOCK].astype(o_ref.dtype))


def _attn_c(proj, t5_blocks, sink, mix, q_blocks=16):
    b, s, _ = proj.shape
    nb = s // Q_BLOCK
    groups = SW_HEADS // SW_KV_HEADS
    qw = groups * HEAD_DIM
    hd = HEAD_DIM
    tq = q_blocks * Q_BLOCK
    return pl.pallas_call(
        functools.partial(_attn_c_kernel, n_blocks=nb, q_blocks=q_blocks),
        grid=(b, SW_KV_HEADS, nb // q_blocks),
        in_specs=[
            pl.BlockSpec((1, tq, qw), lambda bb, kv, i: (bb, i, OFF_CQ // qw + kv)),
            pl.BlockSpec((1, s, hd), lambda bb, kv, i: (bb, 0, OFF_CK // hd + kv)),
            pl.BlockSpec((1, s, hd), lambda bb, kv, i: (bb, 0, OFF_CV // hd + kv)),
            pl.BlockSpec((groups, T5_KINDS, Q_BLOCK, Q_BLOCK), lambda bb, kv, i: (kv, 0, 0, 0)),
            pl.BlockSpec(memory_space=pltpu.SMEM),
            pl.BlockSpec(memory_space=pl.ANY),
        ],
        out_specs=pl.BlockSpec((1, tq, qw), lambda bb, kv, i: (bb, i, MIX_C // qw + kv)),
        out_shape=jax.ShapeDtypeStruct(mix.shape, mix.dtype),
        input_output_aliases={5: 0},
        compiler_params=_cparams(("parallel", "parallel", "arbitrary")),
        name="attn_c",
    )(proj, proj, proj, t5_blocks, sink, mix)


def _attn_d_kernel(q1_ref, q2_ref, k1_ref, k2_ref, v_ref, t5_ref, lam_ref, g_ref, mix_ref, o_ref,
                   *, n_blocks, q_blocks, lambda_init):
    del mix_ref
    lp = lam_ref[...]
    lam = (jnp.exp(jnp.sum(lp[0:1] * lp[1:2], axis=-1, keepdims=True))
           - jnp.exp(jnp.sum(lp[2:3] * lp[3:4], axis=-1, keepdims=True)) + lambda_init)
    k1 = k1_ref[0]
    k2 = k2_ref[0]
    v = v_ref[0]
    for c in range(q_blocks):
        i = pl.program_id(2) * q_blocks + c
        rows = slice(c * Q_BLOCK, (c + 1) * Q_BLOCK)
        bias = jnp.concatenate(
            [t5_ref[0, jnp.clip(kj - i + T5_BAND0 + 1, T5_FAR_LO, T5_FAR_HI)]
             for kj in range(n_blocks)], axis=1)
        if c % DF_GROUP == 0:
            group = slice(c * Q_BLOCK, (c + DF_GROUP) * Q_BLOCK)
            s1 = _dot_nt(q1_ref[0, group, :], k1)
            s2 = _dot_nt(q2_ref[0, group, :], k2)
        sub = slice((c % DF_GROUP) * Q_BLOCK, (c % DF_GROUP + 1) * Q_BLOCK)
        e1, l1 = _softmax_parts(s1[sub] + bias)
        e2, l2 = _softmax_parts(s2[sub] + bias)
        ev = _dot(jnp.concatenate([e1.astype(BF16), e2.astype(BF16)], axis=0), v)
        od = ev[:Q_BLOCK] * (1.0 / l1) - ev[Q_BLOCK:] * (lam / l2)
        o_ref[0, rows, :] = (_rms(od, g_ref[...]) * (1.0 - lambda_init)).astype(o_ref.dtype)


def _attn_d(proj, t5_blocks, lam_params, subln, lambda_init, mix, q_blocks=16):
    b, s, _ = proj.shape
    nb = s // Q_BLOCK
    hd = HEAD_DIM
    tq = q_blocks * Q_BLOCK
    return pl.pallas_call(
        functools.partial(_attn_d_kernel, n_blocks=nb, q_blocks=q_blocks, lambda_init=lambda_init),
        grid=(b, DF_HEADS, nb // q_blocks),
        in_specs=[
            pl.BlockSpec((1, tq, hd), lambda bb, h, i: (bb, i, OFF_DQ // hd + h)),
            pl.BlockSpec((1, tq, hd), lambda bb, h, i: (bb, i, OFF_DQ // hd + DF_HEADS + h)),
            pl.BlockSpec((1, s, hd), lambda bb, h, i: (bb, 0, OFF_DK // hd + h)),
            pl.BlockSpec((1, s, hd), lambda bb, h, i: (bb, 0, OFF_DK // hd + DF_HEADS + h)),
            pl.BlockSpec((1, s, DF_V_DIM), lambda bb, h, i: (bb, 0, OFF_DV // DF_V_DIM + h)),
            pl.BlockSpec((1, T5_KINDS, Q_BLOCK, Q_BLOCK), lambda bb, h, i: (SW_HEADS + h, 0, 0, 0)),
            pl.BlockSpec((4, hd), lambda bb, h, i: (0, 0)),
            pl.BlockSpec((1, DF_V_DIM), lambda bb, h, i: (0, 0)),
            pl.BlockSpec(memory_space=pl.ANY),
        ],
        out_specs=pl.BlockSpec((1, tq, DF_V_DIM), lambda bb, h, i: (bb, i, MIX_D // DF_V_DIM + h)),
        out_shape=jax.ShapeDtypeStruct(mix.shape, mix.dtype),
        input_output_aliases={8: 0},
        compiler_params=_cparams(("parallel", "parallel", "arbitrary")),
        name="attn_d",
    )(proj, proj, proj, proj, proj, t5_blocks, lam_params, subln.reshape(1, -1), mix)


def _rope_tables(s):
    pos = jnp.arange(s, dtype=jnp.int32)
    row = (pos // GRID_W).astype(F32)
    col = (pos % GRID_W).astype(F32)
    n_pairs = HEAD_DIM // 4
    inv = ROPE_THETA ** (-jnp.arange(n_pairs, dtype=F32) / n_pairs)
    ang = jnp.concatenate([row[:, None] * inv, col[:, None] * inv], axis=-1)
    cos = jnp.repeat(jnp.cos(ang), 2, axis=-1)
    sin = jnp.repeat(jnp.sin(ang), 2, axis=-1)
    sign = jnp.tile(jnp.asarray([-1.0, 1.0], F32), HEAD_DIM // 2)
    return cos, sin * sign


def _q_col_scale():
    col = jnp.arange(D_IN, dtype=jnp.int32)
    is_q = (((col >= OFF_AQ) & (col < OFF_AK)) | ((col >= OFF_CQ) & (col < OFF_CK))
            | ((col >= OFF_DQ) & (col < OFF_DK)))
    return jnp.where(is_q, LOGIT_SCALE, 1.0).astype(F32).reshape(1, D_IN)


def kernel(x, ln_attn_pre, ln_attn_post, ln_mlp_pre, ln_mlp_post, w_in, w_out, na_rpb,
           ax_q_norm, ax_k_norm, sw_sink, df_lambda, df_subln, t5_table, w_mlp_in, w_mlp_out):
    b, s, d = x.shape
    depth = w_in.shape[0]
    t = b * s
    cos, sin_signed = _rope_tables(s)
    col_scale = _q_col_scale()
    t5_blocks = _t5_blocks(t5_table)
    xf = x.reshape(t, d)
    h = _norm_cast(xf, ln_attn_pre[0])
    for l in range(depth):
        proj = _matmul(h, w_in, l, BF16, col_scale=col_scale, name="in_proj").reshape(b, s, D_IN)
        qn_b, kn_b = _prep(proj, cos, sin_signed, ax_q_norm[l], ax_k_norm[l])
        mix = _attn_a(proj, _na_table(na_rpb[l]))
        mix = _attn_b(qn_b, kn_b, proj, mix)
        mix = _attn_c(proj, t5_blocks, sw_sink[l], mix)
        lambda_init = 0.8 - 0.6 * math.exp(-0.3 * l)
        mix = _attn_d(proj, t5_blocks, df_lambda[l], df_subln[l], lambda_init, mix)
        y = _matmul(mix.reshape(t, d), w_out, l, BF16, name="out_proj")
        xf, h = _resid_norm(xf, y, ln_attn_post[l], ln_mlp_pre[l])
        u = _matmul(h, w_mlp_in, l, BF16, act=_relu2, name="mlp_in")
        y = _matmul_ksplit(u, w_mlp_out, l, BF16, name="mlp_out")
        g_next = ln_attn_pre[l + 1] if l + 1 < depth else None
        xf, h = _resid_norm(xf, y, ln_mlp_post[l], g_next)
    return xf.reshape(b, s, d)
```

```python
import functools
import math

import jax
import jax.numpy as jnp
from jax import lax
from jax.experimental import pallas as pl
from jax.experimental.pallas import tpu as pltpu

F32 = jnp.float32
BF16 = jnp.bfloat16

D_MODEL = 4096
GRID_W = 64
HEAD_DIM = 128
Q_BLOCK = 128
GROUP_W = D_MODEL // 4
NA_HEADS = GROUP_W // HEAD_DIM
NA_WIN_ROWS = 8
NA_WIN_COLS = 16
AX_HEADS = GROUP_W // HEAD_DIM
AX_KV_HEADS = AX_HEADS // 4
ROPE_THETA = 10000.0
SW_HEADS = GROUP_W // HEAD_DIM
SW_KV_HEADS = SW_HEADS // 4
SW_WINDOW = 128
DF_V_DIM = 2 * HEAD_DIM
DF_HEADS = GROUP_W // DF_V_DIM
T5_BUCKETS = 32
T5_MAX_DIST = 128
T5_HEADS = SW_HEADS + DF_HEADS
EPS = 1e-6
SCALE = HEAD_DIM ** -0.5
LOG2E = math.log2(math.e)
LOGIT_SCALE = SCALE * LOG2E

_IN_WIDTHS = (
    GROUP_W, GROUP_W, GROUP_W,
    GROUP_W, AX_KV_HEADS * HEAD_DIM, AX_KV_HEADS * HEAD_DIM,
    GROUP_W, SW_KV_HEADS * HEAD_DIM, SW_KV_HEADS * HEAD_DIM,
    2 * DF_HEADS * HEAD_DIM, 2 * DF_HEADS * HEAD_DIM, DF_HEADS * DF_V_DIM,
)
(OFF_AQ, OFF_AK, OFF_AV, OFF_BQ, OFF_BK, OFF_BV,
 OFF_CQ, OFF_CK, OFF_CV, OFF_DQ, OFF_DK, OFF_DV) = (
    sum(_IN_WIDTHS[:n]) for n in range(len(_IN_WIDTHS)))
D_IN = sum(_IN_WIDTHS)
MIX_A, MIX_B, MIX_C, MIX_D = 0, GROUP_W, 2 * GROUP_W, 3 * GROUP_W

NA_SPAN_BLOCKS = 5
NA_REL_BLOCKS = 9
NA_GROUP = 4
DF_GROUP = 4
AX_KEY_CHUNK = 512
NA_RPB_ROWS = 2 * NA_WIN_ROWS - 1
NA_RPB_COLS = 2 * NA_WIN_COLS - 1
NA_RPB_ROWS_PADDED = 16

T5_FAR_LO, T5_BAND0, T5_FAR_HI, T5_MASKED, T5_KINDS = 0, 1, 4, 5, 6

V7X_VMEM_BYTES = 64 * 1024 * 1024
VMEM_LIMIT = V7X_VMEM_BYTES - 4 * 1024 * 1024
NEG_INF = float("-inf")


def _cparams(sem):
    return pltpu.CompilerParams(dimension_semantics=sem, vmem_limit_bytes=VMEM_LIMIT)


def _dot(a, b):
    return jnp.dot(a, b, preferred_element_type=F32)


def _dot_nt(a, b):
    return lax.dot_general(a, b, (((1,), (1,)), ((), ())), preferred_element_type=F32)


def _rms(x, g):
    return x * lax.rsqrt(jnp.mean(x * x, axis=-1, keepdims=True) + EPS) * g


def _softmax_parts(s, extra=None):
    m = jnp.max(s, axis=-1, keepdims=True)
    if extra is not None:
        m = jnp.maximum(m, extra)
    e = jnp.exp2(s - m)
    l = jnp.sum(e, axis=-1, keepdims=True)
    if extra is not None:
        l = l + jnp.exp2(extra - m)
    return e, l


def _norm_cast_kernel(x_ref, g_ref, h_ref):
    h_ref[...] = _rms(x_ref[...], g_ref[...]).astype(h_ref.dtype)


def _norm_cast(x, g, rows=512):
    t, d = x.shape
    return pl.pallas_call(
        _norm_cast_kernel,
        grid=(t // rows,),
        in_specs=[pl.BlockSpec((rows, d), lambda i: (i, 0)),
                  pl.BlockSpec((1, d), lambda i: (0, 0))],
        out_specs=pl.BlockSpec((rows, d), lambda i: (i, 0)),
        out_shape=jax.ShapeDtypeStruct((t, d), BF16),
        compiler_params=_cparams(("parallel",)),
        name="norm_cast",
    )(x, g.reshape(1, d))


def _resid_norm_kernel(x_ref, y_ref, gp_ref, gn_ref, xo_ref, h_ref):
    xn = x_ref[...] + _rms(y_ref[...].astype(F32), gp_ref[...])
    xo_ref[...] = xn
    h_ref[...] = _rms(xn, gn_ref[...]).astype(h_ref.dtype)


def _resid_kernel(x_ref, y_ref, gp_ref, xo_ref):
    xo_ref[...] = x_ref[...] + _rms(y_ref[...].astype(F32), gp_ref[...])


def _resid_norm(x, y, g_post, g_next, rows=512):
    t, d = x.shape
    row_spec = pl.BlockSpec((rows, d), lambda i: (i, 0))
    g_spec = pl.BlockSpec((1, d), lambda i: (0, 0))
    if g_next is None:
        return pl.pallas_call(
            _resid_kernel, grid=(t // rows,),
            in_specs=[row_spec, row_spec, g_spec], out_specs=row_spec,
            out_shape=jax.ShapeDtypeStruct((t, d), F32),
            compiler_params=_cparams(("parallel",)), name="resid",
        )(x, y, g_post.reshape(1, d)), None
    return pl.pallas_call(
        _resid_norm_kernel, grid=(t // rows,),
        in_specs=[row_spec, row_spec, g_spec, g_spec], out_specs=[row_spec, row_spec],
        out_shape=[jax.ShapeDtypeStruct((t, d), F32), jax.ShapeDtypeStruct((t, d), BF16)],
        compiler_params=_cparams(("parallel",)), name="resid_norm",
    )(x, y, g_post.reshape(1, d), g_next.reshape(1, d))


def _relu2(r):
    r = jnp.maximum(r, 0.0)
    return r * r


def _mm_kernel(a_ref, w_ref, *rest, act, col_scaled):
    o_ref = rest[-1]
    r = _dot(a_ref[...], w_ref[...].astype(BF16))
    if act is not None:
        r = act(r)
    if col_scaled:
        r = r * rest[0][...]
    o_ref[...] = r.astype(o_ref.dtype)


def _matmul(a, w, layer, out_dtype, *, tm=2048, tn=512, act=None, col_scale=None, name="matmul"):
    m, kdim = a.shape
    _, _, n = w.shape
    in_specs = [pl.BlockSpec((tm, kdim), lambda i, j: (i, 0)),
                pl.BlockSpec((None, kdim, tn), lambda i, j: (layer, 0, j))]
    args = [a, w]
    if col_scale is not None:
        in_specs.append(pl.BlockSpec((1, tn), lambda i, j: (0, j)))
        args.append(col_scale)
    return pl.pallas_call(
        functools.partial(_mm_kernel, act=act, col_scaled=col_scale is not None),
        grid=(m // tm, n // tn),
        in_specs=in_specs,
        out_specs=pl.BlockSpec((tm, tn), lambda i, j: (i, j)),
        out_shape=jax.ShapeDtypeStruct((m, n), out_dtype),
        compiler_params=_cparams(("parallel", "parallel")),
        name=name,
    )(*args)


def _mm_ksplit_kernel(a_ref, w_ref, o_ref, acc_ref):
    k = pl.program_id(2)

    @pl.when(k == 0)
    def _():
        acc_ref[...] = jnp.zeros(acc_ref.shape, acc_ref.dtype)

    acc_ref[...] += _dot(a_ref[...], w_ref[...].astype(BF16))

    @pl.when(k == pl.num_programs(2) - 1)
    def _():
        o_ref[...] = acc_ref[...].astype(o_ref.dtype)


def _matmul_ksplit(a, w, layer, out_dtype, *, tm=2048, tn=1024, tk=2048, name="matmul_ksplit"):
    m, kdim = a.shape
    _, _, n = w.shape
    return pl.pallas_call(
        _mm_ksplit_kernel,
        grid=(m // tm, n // tn, kdim // tk),
        in_specs=[pl.BlockSpec((tm, tk), lambda i, j, k: (i, k)),
                  pl.BlockSpec((None, tk, tn), lambda i, j, k: (layer, k, j))],
        out_specs=pl.BlockSpec((tm, tn), lambda i, j, k: (i, j)),
        out_shape=jax.ShapeDtypeStruct((m, n), out_dtype),
        scratch_shapes=[pltpu.VMEM((tm, tn), F32)],
        compiler_params=_cparams(("parallel", "parallel", "arbitrary")),
        name=name,
    )(a, w)


def _t5_bucket(rel):
    nb = T5_BUCKETS // 2
    max_exact = nb // 2
    base = jnp.where(rel > 0, nb, 0)
    n = jnp.abs(rel)
    n_f = jnp.maximum(n, 1).astype(F32)
    large = max_exact + (jnp.log(n_f / max_exact) / math.log(T5_MAX_DIST / max_exact)
                         * (nb - max_exact)).astype(jnp.int32)
    large = jnp.minimum(large, nb - 1)
    return base + jnp.where(n < max_exact, n, large)


def _t5_blocks_kernel(idx_ref, tab_ref, o_ref):
    h = pl.program_id(0)
    shape = (Q_BLOCK, Q_BLOCK)
    t = lax.broadcasted_iota(jnp.int32, shape, 0)
    u = lax.broadcasted_iota(jnp.int32, shape, 1)
    o_ref[0, T5_FAR_LO] = jnp.full(shape, tab_ref[T5_BUCKETS // 2 - 1, h] * LOG2E, F32)
    o_ref[0, T5_FAR_HI] = jnp.full(shape, tab_ref[T5_BUCKETS - 1, h] * LOG2E, F32)
    o_ref[0, T5_MASKED] = jnp.full(shape, NEG_INF, F32)
    for j in range(3):
        idx = idx_ref[j]
        diag = jnp.zeros(idx.shape, F32)
        for b in range(T5_BUCKETS):
            diag = jnp.where(idx == b, tab_ref[b, h], diag)
        upper = pltpu.roll(jnp.broadcast_to(diag[0:1], shape), 0, 1, stride=1, stride_axis=0)
        lower = pltpu.roll(jnp.broadcast_to(diag[1:2], shape), 0, 1, stride=1, stride_axis=0)
        band = jnp.where(u >= t, upper, lower)
        rel = u + (j - 1) * Q_BLOCK - t
        allowed = (jnp.abs(rel) <= SW_WINDOW) | (h >= SW_HEADS)
        o_ref[0, T5_BAND0 + j] = jnp.where(allowed, band, NEG_INF) * LOG2E


def _t5_blocks(t5_table):
    k = jnp.arange(Q_BLOCK, dtype=jnp.int32)
    block_off = (jnp.arange(3, dtype=jnp.int32) - 1) * Q_BLOCK
    rel = block_off[:, None, None] + jnp.stack([k, k - Q_BLOCK])[None]
    idx = _t5_bucket(rel).astype(jnp.int32)
    return pl.pallas_call(
        _t5_blocks_kernel,
        grid=(T5_HEADS,),
        in_specs=[pl.BlockSpec((3, 2, Q_BLOCK), lambda h: (0, 0, 0)),
                  pl.BlockSpec(memory_space=pltpu.SMEM)],
        out_specs=pl.BlockSpec((1, T5_KINDS, Q_BLOCK, Q_BLOCK), lambda h: (h, 0, 0, 0)),
        out_shape=jax.ShapeDtypeStruct((T5_HEADS, T5_KINDS, Q_BLOCK, Q_BLOCK), F32),
        compiler_params=_cparams(("arbitrary",)),
        name="t5_blocks",
    )(idx, t5_table)


def _na_table_kernel(rpb_ref, o_ref):
    shape = (GRID_W, Q_BLOCK)
    c = lax.broadcasted_iota(jnp.int32, shape, 0)
    u = lax.broadcasted_iota(jnp.int32, shape, 1)
    k_hi = u >= GRID_W
    kc = jnp.bitwise_and(u, GRID_W - 1)
    cs = jnp.clip(c - NA_WIN_COLS // 2, 0, GRID_W - NA_WIN_COLS)
    col_ok = (kc >= cs) & (kc < cs + NA_WIN_COLS)

    def toeplitz(dr, lane_off):
        if abs(dr) > NA_WIN_ROWS - 1:
            return jnp.full(shape, NEG_INF, F32)
        row = dr + NA_WIN_ROWS - 1
        vec = jnp.broadcast_to(rpb_ref[0, row:row + 1, :], shape)
        shift = (lane_off - (NA_WIN_COLS - 1)) % Q_BLOCK
        return pltpu.roll(vec, shift, 1, stride=1, stride_axis=0)

    for d in range(NA_REL_BLOCKS):
        dblk = d - NA_REL_BLOCKS // 2
        halves = []
        for a in range(2):
            even = toeplitz(2 * dblk - a, 0)
            odd = toeplitz(2 * dblk + 1 - a, GRID_W)
            halves.append(jnp.where(col_ok, jnp.where(k_hi, odd, even), NEG_INF))
        o_ref[0, d] = jnp.concatenate(halves, axis=0) * LOG2E


def _na_table(rpb):
    heads, n_rows, n_cols = rpb.shape
    padded = jnp.pad(rpb, ((0, 0), (0, NA_RPB_ROWS_PADDED - n_rows), (0, Q_BLOCK - n_cols)))
    return pl.pallas_call(
        _na_table_kernel,
        grid=(heads,),
        in_specs=[pl.BlockSpec((1, NA_RPB_ROWS_PADDED, Q_BLOCK), lambda h: (h, 0, 0))],
        out_specs=pl.BlockSpec((1, NA_REL_BLOCKS, Q_BLOCK, Q_BLOCK), lambda h: (h, 0, 0, 0)),
        out_shape=jax.ShapeDtypeStruct((heads, NA_REL_BLOCKS, Q_BLOCK, Q_BLOCK), F32),
        compiler_params=_cparams(("arbitrary",)),
        name="na_table",
    )(padded)


def _rope(x, cos, sin_signed):
    lane = lax.broadcasted_iota(jnp.int32, x.shape, 1)
    even = jnp.bitwise_and(lane, 1) == 0
    swapped = jnp.where(even, pltpu.roll(x, HEAD_DIM - 1, 1), pltpu.roll(x, 1, 1))
    return x * cos + swapped * sin_signed


def _prep_kernel(bq_ref, bk_ref, cos_ref, sin_ref, qg_ref, kg_ref, qo_ref, ko_ref):
    cos = cos_ref[...]
    sin = sin_ref[...]
    for h in range(AX_HEADS):
        cols = slice(h * HEAD_DIM, (h + 1) * HEAD_DIM)
        qh = _rope(_rms(bq_ref[0, :, cols].astype(F32), qg_ref[...]), cos, sin)
        qo_ref[0, :, cols] = (qh * LOGIT_SCALE).astype(qo_ref.dtype)
    for h in range(AX_KV_HEADS):
        cols = slice(h * HEAD_DIM, (h + 1) * HEAD_DIM)
        kh = _rope(_rms(bk_ref[0, :, cols].astype(F32), kg_ref[...]), cos, sin)
        ko_ref[0, :, cols] = kh.astype(ko_ref.dtype)


def _prep(proj, cos, sin_signed, q_gain, k_gain, rows=512):
    b, s, _ = proj.shape
    kw = AX_KV_HEADS * HEAD_DIM
    return pl.pallas_call(
        _prep_kernel,
        grid=(b, s // rows),
        in_specs=[
            pl.BlockSpec((1, rows, GROUP_W), lambda bb, r: (bb, r, OFF_BQ // GROUP_W)),
            pl.BlockSpec((1, rows, kw), lambda bb, r: (bb, r, OFF_BK // kw)),
            pl.BlockSpec((rows, HEAD_DIM), lambda bb, r: (r, 0)),
            pl.BlockSpec((rows, HEAD_DIM), lambda bb, r: (r, 0)),
            pl.BlockSpec((1, HEAD_DIM), lambda bb, r: (0, 0)),
            pl.BlockSpec((1, HEAD_DIM), lambda bb, r: (0, 0)),
        ],
        out_specs=[pl.BlockSpec((1, rows, GROUP_W), lambda bb, r: (bb, r, 0)),
                   pl.BlockSpec((1, rows, kw), lambda bb, r: (bb, r, 0))],
        out_shape=[jax.ShapeDtypeStruct((b, s, GROUP_W), BF16),
                   jax.ShapeDtypeStruct((b, s, kw), BF16)],
        compiler_params=_cparams(("parallel", "parallel")),
        name="prep",
    )(proj, proj, cos, sin_signed, q_gain.reshape(1, -1), k_gain.reshape(1, -1))


def _na_row_penalty(i, a, key_block, grid_rows, key_is_odd):
    first = min(max(2 * i + a - NA_WIN_ROWS // 2, 0), grid_rows - NA_WIN_ROWS)
    ok = [first <= 2 * key_block + half < first + NA_WIN_ROWS for half in range(2)]
    if all(ok):
        return None
    if not any(ok):
        return jnp.full((1, Q_BLOCK), NEG_INF, F32)
    return jnp.where(key_is_odd == ok[1], 0.0, NEG_INF)


def _attn_a_kernel(q_ref, k_ref, v_ref, tb_ref, o_ref, *, n_blocks):
    grid_rows = 2 * n_blocks
    union = NA_GROUP + NA_SPAN_BLOCKS - 1
    key_is_odd = lax.broadcasted_iota(jnp.int32, (1, Q_BLOCK), 1) >= GRID_W
    for i0 in range(0, n_blocks, NA_GROUP):
        su = min(max(i0 - 2, 0), n_blocks - union)
        keys = slice(su * Q_BLOCK, (su + union) * Q_BLOCK)
        s_union = _dot_nt(q_ref[0, i0 * Q_BLOCK:(i0 + NA_GROUP) * Q_BLOCK, :], k_ref[0, keys, :])
        e_rows, sums = [], []
        for c in range(NA_GROUP):
            i = i0 + c
            sb = min(max(i - 2, 0), n_blocks - NA_SPAN_BLOCKS)
            off = sb - su
            pieces = []
            for j in range(NA_SPAN_BLOCKS):
                blk = tb_ref[0, sb + j - i + NA_REL_BLOCKS // 2]
                halves = []
                for a in range(2):
                    half = blk[a * GRID_W:(a + 1) * GRID_W]
                    pen = _na_row_penalty(i, a, sb + j, grid_rows, key_is_odd)
                    halves.append(half if pen is None else half + pen)
                pieces.append(jnp.concatenate(halves, axis=0))
            s = s_union[c * Q_BLOCK:(c + 1) * Q_BLOCK, off * Q_BLOCK:(off + NA_SPAN_BLOCKS) * Q_BLOCK]
            e, l = _softmax_parts(s + jnp.concatenate(pieces, axis=1))
            pad = [jnp.zeros((Q_BLOCK, n * Q_BLOCK), BF16)
                   for n in (off, union - NA_SPAN_BLOCKS - off)]
            e_rows.append(jnp.concatenate(
                [p for p in (pad[0], e.astype(BF16), pad[1]) if p.shape[1]], axis=1))
            sums.append(l)
        o = _dot(jnp.concatenate(e_rows, axis=0), v_ref[0, keys, :])
        for c in range(NA_GROUP):
            rows = slice((i0 + c) * Q_BLOCK, (i0 + c + 1) * Q_BLOCK)
            o_ref[0, rows, :] = (o[c * Q_BLOCK:(c + 1) * Q_BLOCK] * (1.0 / sums[c])).astype(o_ref.dtype)


def _attn_a(proj, table):
    b, s, _ = proj.shape
    nb = s // Q_BLOCK
    hd = HEAD_DIM
    return pl.pallas_call(
        functools.partial(_attn_a_kernel, n_blocks=nb),
        grid=(NA_HEADS, b),
        in_specs=[
            pl.BlockSpec((1, s, hd), lambda h, bb: (bb, 0, OFF_AQ // hd + h)),
            pl.BlockSpec((1, s, hd), lambda h, bb: (bb, 0, OFF_AK // hd + h)),
            pl.BlockSpec((1, s, hd), lambda h, bb: (bb, 0, OFF_AV // hd + h)),
            pl.BlockSpec((1, NA_REL_BLOCKS, Q_BLOCK, Q_BLOCK), lambda h, bb: (h, 0, 0, 0)),
        ],
        out_specs=pl.BlockSpec((1, s, hd), lambda h, bb: (bb, 0, MIX_A // hd + h)),
        out_shape=jax.ShapeDtypeStruct((b, s, D_MODEL), BF16),
        compiler_params=_cparams(("parallel", "parallel")),
        name="attn_a",
    )(proj, proj, proj, table)


def _attn_b_kernel(q_ref, k_ref, v_ref, mix_ref, o_ref):
    del mix_ref
    k = k_ref[0]
    v = v_ref[0]
    n_chunks = k.shape[0] // AX_KEY_CHUNK
    for g in range(AX_HEADS // AX_KV_HEADS):
        cols = slice(g * HEAD_DIM, (g + 1) * HEAD_DIM)
        q = q_ref[0, :, cols]
        m = l = acc = None
        for c in range(n_chunks):
            keys = slice(c * AX_KEY_CHUNK, (c + 1) * AX_KEY_CHUNK)
            s = _dot_nt(q, k[keys])
            m_c = jnp.max(s, axis=-1, keepdims=True)
            if c == 0:
                m = m_c
                e = jnp.exp2(s - m)
                l = jnp.sum(e, axis=-1, keepdims=True)
                acc = _dot(e.astype(BF16), v[keys])
            else:
                m_new = jnp.maximum(m, m_c)
                alpha = jnp.exp2(m - m_new)
                e = jnp.exp2(s - m_new)
                l = l * alpha + jnp.sum(e, axis=-1, keepdims=True)
                acc = acc * alpha + _dot(e.astype(BF16), v[keys])
                m = m_new
        o_ref[0, :, cols] = (acc * (1.0 / l)).astype(o_ref.dtype)


def _attn_b(qn, kn, proj, mix, tq=1024):
    b, s, _ = proj.shape
    qw = (AX_HEADS // AX_KV_HEADS) * HEAD_DIM
    return pl.pallas_call(
        _attn_b_kernel,
        grid=(b, AX_KV_HEADS, s // tq),
        in_specs=[
            pl.BlockSpec((1, tq, qw), lambda bb, kv, i: (bb, i, kv)),
            pl.BlockSpec((1, s, HEAD_DIM), lambda bb, kv, i: (bb, 0, kv)),
            pl.BlockSpec((1, s, HEAD_DIM), lambda bb, kv, i: (bb, 0, OFF_BV // HEAD_DIM + kv)),
            pl.BlockSpec(memory_space=pl.ANY),
        ],
        out_specs=pl.BlockSpec((1, tq, qw), lambda bb, kv, i: (bb, i, MIX_B // qw + kv)),
        out_shape=jax.ShapeDtypeStruct(mix.shape, mix.dtype),
        input_output_aliases={3: 0},
        compiler_params=_cparams(("parallel", "parallel", "arbitrary")),
        name="attn_b",
    )(qn, kn, proj, mix)


def _attn_c_kernel(q_ref, k_ref, v_ref, t5_ref, sink_ref, mix_ref, o_ref, *, n_blocks, q_blocks):
    del mix_ref
    kv = pl.program_id(1)
    groups = SW_HEADS // SW_KV_HEADS
    sink = jnp.concatenate(
        [jnp.full((Q_BLOCK, 1), sink_ref[kv * groups + g] * LOG2E, F32) for g in range(groups)],
        axis=0)
    for c in range(q_blocks):
        i = pl.program_id(2) * q_blocks + c
        rows = slice(c * Q_BLOCK, (c + 1) * Q_BLOCK)
        blocks = (jnp.maximum(i - 1, 0), i, jnp.minimum(i + 1, n_blocks - 1))
        kinds = (jnp.where(i > 0, T5_BAND0, T5_MASKED), T5_BAND0 + 1,
                 jnp.where(i < n_blocks - 1, T5_BAND0 + 2, T5_MASKED))
        starts = [pl.multiple_of(blk * Q_BLOCK, Q_BLOCK) for blk in blocks]
        q = jnp.concatenate(
            [q_ref[0, rows, g * HEAD_DIM:(g + 1) * HEAD_DIM] for g in range(groups)], axis=0)
        s = jnp.concatenate(
            [_dot_nt(q, k_ref[0, pl.ds(start, Q_BLOCK), :])
             + jnp.concatenate([t5_ref[g, kind] for g in range(groups)], axis=0)
             for start, kind in zip(starts, kinds)], axis=1)
        e, l = _softmax_parts(s, extra=sink)
        v = jnp.concatenate([v_ref[0, pl.ds(start, Q_BLOCK), :] for start in starts], axis=0)
        o = _dot(e.astype(BF16), v) * (1.0 / l)
        for g in range(groups):
            o_ref[0, rows, g * HEAD_DIM:(g + 1) * HEAD_DIM] = (
                o[g * Q_BLOCK:(g + 1) * Q_BLOCK].astype(o_ref.dtype))


def _attn_c(proj, t5_blocks, sink, mix, q_blocks=16):
    b, s, _ = proj.shape
    nb = s // Q_BLOCK
    groups = SW_HEADS // SW_KV_HEADS
    qw = groups * HEAD_DIM
    hd = HEAD_DIM
    tq = q_blocks * Q_BLOCK
    return pl.pallas_call(
        functools.partial(_attn_c_kernel, n_blocks=nb, q_blocks=q_blocks),
        grid=(b, SW_KV_HEADS, nb // q_blocks),
        in_specs=[
            pl.BlockSpec((1, tq, qw), lambda bb, kv, i: (bb, i, OFF_CQ // qw + kv)),
            pl.BlockSpec((1, s, hd), lambda bb, kv, i: (bb, 0, OFF_CK // hd + kv)),
            pl.BlockSpec((1, s, hd), lambda bb, kv, i: (bb, 0, OFF_CV // hd + kv)),
            pl.BlockSpec((groups, T5_KINDS, Q_BLOCK, Q_BLOCK), lambda bb, kv, i: (kv, 0, 0, 0)),
            pl.BlockSpec(memory_space=pltpu.SMEM),
            pl.BlockSpec(memory_space=pl.ANY),
        ],
        out_specs=pl.BlockSpec((1, tq, qw), lambda bb, kv, i: (bb, i, MIX_C // qw + kv)),
        out_shape=jax.ShapeDtypeStruct(mix.shape, mix.dtype),
        input_output_aliases={5: 0},
        compiler_params=_cparams(("parallel", "parallel", "arbitrary")),
        name="attn_c",
    )(proj, proj, proj, t5_blocks, sink, mix)


def _attn_d_kernel(q1_ref, q2_ref, k1_ref, k2_ref, v_ref, t5_ref, lam_ref, g_ref, mix_ref, o_ref,
                   *, n_blocks, q_blocks, lambda_init):
    del mix_ref
    lp = lam_ref[...]
    lam = (jnp.exp(jnp.sum(lp[0:1] * lp[1:2], axis=-1, keepdims=True))
           - jnp.exp(jnp.sum(lp[2:3] * lp[3:4], axis=-1, keepdims=True)) + lambda_init)
    k1 = k1_ref[0]
    k2 = k2_ref[0]
    v = v_ref[0]
    for c in range(q_blocks):
        i = pl.program_id(2) * q_blocks + c
        rows = slice(c * Q_BLOCK, (c + 1) * Q_BLOCK)
        bias = jnp.concatenate(
            [t5_ref[0, jnp.clip(kj - i + T5_BAND0 + 1, T5_FAR_LO, T5_FAR_HI)]
             for kj in range(n_blocks)], axis=1)
        if c % DF_GROUP == 0:
            group = slice(c * Q_BLOCK, (c + DF_GROUP) * Q_BLOCK)
            s1 = _dot_nt(q1_ref[0, group, :], k1)
            s2 = _dot_nt(q2_ref[0, group, :], k2)
        sub = slice((c % DF_GROUP) * Q_BLOCK, (c % DF_GROUP + 1) * Q_BLOCK)
        e1, l1 = _softmax_parts(s1[sub] + bias)
        e2, l2 = _softmax_parts(s2[sub] + bias)
        ev = _dot(jnp.concatenate([e1.astype(BF16), e2.astype(BF16)], axis=0), v)
        od = ev[:Q_BLOCK] * (1.0 / l1) - ev[Q_BLOCK:] * (lam / l2)
        o_ref[0, rows, :] = (_rms(od, g_ref[...]) * (1.0 - lambda_init)).astype(o_ref.dtype)


def _attn_d(proj, t5_blocks, lam_params, subln, lambda_init, mix, q_blocks=16):
    b, s, _ = proj.shape
    nb = s // Q_BLOCK
    hd = HEAD_DIM
    tq = q_blocks * Q_BLOCK
    return pl.pallas_call(
        functools.partial(_attn_d_kernel, n_blocks=nb, q_blocks=q_blocks, lambda_init=lambda_init),
        grid=(b, DF_HEADS, nb // q_blocks),
        in_specs=[
            pl.BlockSpec((1, tq, hd), lambda bb, h, i: (bb, i, OFF_DQ // hd + h)),
            pl.BlockSpec((1, tq, hd), lambda bb, h, i: (bb, i, OFF_DQ // hd + DF_HEADS + h)),
            pl.BlockSpec((1, s, hd), lambda bb, h, i: (bb, 0, OFF_DK // hd + h)),
            pl.BlockSpec((1, s, hd), lambda bb, h, i: (bb, 0, OFF_DK // hd + DF_HEADS + h)),
            pl.BlockSpec((1, s, DF_V_DIM), lambda bb, h, i: (bb, 0, OFF_DV // DF_V_DIM + h)),
            pl.BlockSpec((1, T5_KINDS, Q_BLOCK, Q_BLOCK), lambda bb, h, i: (SW_HEADS + h, 0, 0, 0)),
            pl.BlockSpec((4, hd), lambda bb, h, i: (0, 0)),
            pl.BlockSpec((1, DF_V_DIM), lambda bb, h, i: (0, 0)),
            pl.BlockSpec(memory_space=pl.ANY),
        ],
        out_specs=pl.BlockSpec((1, tq, DF_V_DIM), lambda bb, h, i: (bb, i, MIX_D // DF_V_DIM + h)),
        out_shape=jax.ShapeDtypeStruct(mix.shape, mix.dtype),
        input_output_aliases={8: 0},
        compiler_params=_cparams(("parallel", "parallel", "arbitrary")),
        name="attn_d",
    )(proj, proj, proj, proj, proj, t5_blocks, lam_params, subln.reshape(1, -1), mix)


def _rope_tables(s):
    pos = jnp.arange(s, dtype=jnp.int32)
    row = (pos // GRID_W).astype(F32)
    col = (pos % GRID_W).astype(F32)
    n_pairs = HEAD_DIM // 4
    inv = ROPE_THETA ** (-jnp.arange(n_pairs, dtype=F32) / n_pairs)
    ang = jnp.concatenate([row[:, None] * inv, col[:, None] * inv], axis=-1)
    cos = jnp.repeat(jnp.cos(ang), 2, axis=-1)
    sin = jnp.repeat(jnp.sin(ang), 2, axis=-1)
    sign = jnp.tile(jnp.asarray([-1.0, 1.0], F32), HEAD_DIM // 2)
    return cos, sin * sign


def _q_col_scale():
    col = jnp.arange(D_IN, dtype=jnp.int32)
    is_q = (((col >= OFF_AQ) & (col < OFF_AK)) | ((col >= OFF_CQ) & (col < OFF_CK))
            | ((col >= OFF_DQ) & (col < OFF_DK)))
    return jnp.where(is_q, LOGIT_SCALE, 1.0).astype(F32).reshape(1, D_IN)


def kernel(x, ln_attn_pre, ln_attn_post, ln_mlp_pre, ln_mlp_post, w_in, w_out, na_rpb,
           ax_q_norm, ax_k_norm, sw_sink, df_lambda, df_subln, t5_table, w_mlp_in, w_mlp_out):
    b, s, d = x.shape
    depth = w_in.shape[0]
    t = b * s
    cos, sin_signed = _rope_tables(s)
    col_scale = _q_col_scale()
    t5_blocks = _t5_blocks(t5_table)
    xf = x.reshape(t, d)
    h = _norm_cast(xf, ln_attn_pre[0])
    for l in range(depth):
        proj = _matmul(h, w_in, l, BF16, col_scale=col_scale, name="in_proj").reshape(b, s, D_IN)
        qn_b, kn_b = _prep(proj, cos, sin_signed, ax_q_norm[l], ax_k_norm[l])
        mix = _attn_a(proj, _na_table(na_rpb[l]))
        mix = _attn_b(qn_b, kn_b, proj, mix)
        mix = _attn_c(proj, t5_blocks, sw_sink[l], mix)
        lambda_init = 0.8 - 0.6 * math.exp(-0.3 * l)
        mix = _attn_d(proj, t5_blocks, df_lambda[l], df_subln[l], lambda_init, mix)
        y = _matmul(mix.reshape(t, d), w_out, l, BF16, name="out_proj")
        xf, h = _resid_norm(xf, y, ln_attn_post[l], ln_mlp_pre[l])
        u = _matmul(h, w_mlp_in, l, BF16, act=_relu2, name="mlp_in")
        y = _matmul_ksplit(u, w_mlp_out, l, BF16, name="mlp_out")
        g_next = ln_attn_pre[l + 1] if l + 1 < depth else None
        xf, h = _resid_norm(xf, y, ln_mlp_post[l], g_next)
    return xf.reshape(b, s, d)
```
